```python
import jax, jax.numpy as jnp
from jax import lax
import numpy as np

D_MODEL = 1024
BATCH = 32
SEQ = 2048
DEPTH = 4
DEC_BATCH = 32
DEC_SEQ = 64
PAST_LEN = 2048

CHUNK = 64
QBLOCK = 128
N_MIXERS = 4
HEAD_DIM = 64
FOX_HEADS = 16
MLA_HEADS = 16
MLA_Q_RANK = 384
MLA_KV_RANK = 256
MLA_D_NOPE = 64
MLA_D_ROPE = 32
MLA_D_V = 64
ROPE_THETA = 10000.0
SWA_HEADS = 16
SWA_KV_HEADS = 4
SWA_GROUPS = SWA_HEADS // SWA_KV_HEADS
WINDOW = 128
SWA_BAND_CHUNKS = WINDOW // CHUNK
SB_HEADS = 16
N_EXPERTS = 16
N_GROUPS = 4
EXPERTS_PER_GROUP = N_EXPERTS // N_GROUPS
TOP_K = 2
D_EXPERT = 256
DEEPNORM_ALPHA = (2.0 * DEPTH) ** 0.25
DEEPNORM_BETA = (8.0 * DEPTH) ** -0.25
NEG = -1e30

kernel_name = "hybrid_streaming_encoder_step"

F32 = jnp.float32


def layer_norm(x, g, b, eps=1e-5):
    xf = x.astype(F32)
    mu = jnp.mean(xf, axis=-1, keepdims=True)
    var = jnp.mean(jnp.square(xf - mu), axis=-1, keepdims=True)
    return ((xf - mu) * lax.rsqrt(var + eps) * g.astype(F32) + b.astype(F32)).astype(x.dtype)


def rms_norm(x, g, eps=1e-6):
    xf = x.astype(F32)
    return (xf * lax.rsqrt(jnp.mean(jnp.square(xf), axis=-1, keepdims=True) + eps) * g.astype(F32)).astype(x.dtype)


def rope(x, pos):
    d = x.shape[-1]
    half = d // 2
    inv = ROPE_THETA ** (-jnp.arange(half, dtype=F32) * 2.0 / d)
    ang = pos.astype(F32)[:, None] * inv[None, :]
    shp = (1, x.shape[1]) + (1,) * (x.ndim - 3) + (half,)
    cos, sin = jnp.cos(ang).reshape(shp), jnp.sin(ang).reshape(shp)
    x1, x2 = x[..., :half].astype(F32), x[..., half:].astype(F32)
    return jnp.concatenate([x1 * cos - x2 * sin, x2 * cos + x1 * sin], axis=-1).astype(x.dtype)


def alibi_slopes(n_heads):
    return jnp.exp2(-8.0 * jnp.arange(1, n_heads + 1, dtype=F32) / n_heads)


def with_past(past, new):
    return new if past is None else jnp.concatenate([past, new.astype(past.dtype)], axis=1)


def sweep_blocks(block_fn, n_q, key_end):
    outs = []
    for q0 in range(0, n_q, QBLOCK):
        q1 = min(q0 + QBLOCK, n_q)
        outs.append(block_fn(q0, q1, key_end(q1)))
    return outs[0] if len(outs) == 1 else jnp.concatenate(outs, axis=1)


def fox_mixer(h, p, cache):
    B, S, _ = h.shape
    H, hd = FOX_HEADS, HEAD_DIM
    proj = h @ p["fox_w_in"]
    q, k, v, fg = jnp.split(proj, [H * hd, 2 * H * hd, 3 * H * hd], axis=-1)
    q, k, v = q.reshape(B, S, H, hd), k.reshape(B, S, H, hd), v.reshape(B, S, H, hd)
    logf = jax.nn.log_sigmoid((fg + p["fox_b_f"]).astype(F32))
    ck, cv, clf = cache if cache is not None else (None, None, None)
    off = 0 if cache is None else ck.shape[1]
    K, V = with_past(ck, k), with_past(cv, v)
    F = jnp.cumsum(with_past(clf, logf).astype(F32), axis=1)
    Ft = jnp.transpose(F, (0, 2, 1))
    scale = hd ** -0.5

    def block(q0, q1, kend):
        qpos = off + jnp.arange(q0, q1)
        s = jnp.einsum('bqhd,bkhd->bhqk', q[:, q0:q1], K[:, :kend]).astype(F32) * scale
        s = s + Ft[:, :, off + q0:off + q1, None] - Ft[:, :, None, :kend]
        mask = jnp.arange(kend)[None, :] <= qpos[:, None]
        pr = jax.nn.softmax(jnp.where(mask, s, NEG), axis=-1).astype(V.dtype)
        return jnp.einsum('bhqk,bkhd->bqhd', pr, V[:, :kend])

    o = sweep_blocks(block, S, lambda q1: off + q1)
    return o.reshape(B, S, H * hd) @ p["fox_w_out"], (k, v, logf)


def mla_mixer(h, p, cache):
    B, S, _ = h.shape
    H = MLA_HEADS
    proj = h @ p["mla_w_in"]
    cq, ckv, kr = jnp.split(proj, [MLA_Q_RANK, MLA_Q_RANK + MLA_KV_RANK], axis=-1)
    cq = rms_norm(cq, p["mla_q_norm"])
    ckv = rms_norm(ckv, p["mla_kv_norm"])
    q = (cq @ p["mla_w_uq"]).reshape(B, S, H, MLA_D_NOPE + MLA_D_ROPE)
    q_nope, q_rope = q[..., :MLA_D_NOPE], q[..., MLA_D_NOPE:]
    off = 0 if cache is None else cache[0].shape[1]
    pos = off + jnp.arange(S)
    q_rope = rope(q_rope, pos)
    kr = rope(kr, pos)
    q_lat = jnp.einsum('bshn,rhn->bshr', q_nope, p["mla_w_uk"])
    c_ckv, c_kr = cache if cache is not None else (None, None)
    CKV, KR = with_past(c_ckv, ckv), with_past(c_kr, kr)
    n_k = CKV.shape[1]
    scale = (MLA_D_NOPE + MLA_D_ROPE) ** -0.5

    def block(q0, q1, kend):
        qpos = off + jnp.arange(q0, q1)
        s = (jnp.einsum('bqhr,bkr->bhqk', q_lat[:, q0:q1], CKV[:, :kend])
             + jnp.einsum('bqhe,bke->bhqk', q_rope[:, q0:q1], KR[:, :kend])).astype(F32) * scale
        mask = (jnp.arange(kend) // CHUNK)[None, :] <= (qpos // CHUNK)[:, None]
        pr = jax.nn.softmax(jnp.where(mask, s, NEG), axis=-1).astype(CKV.dtype)
        o_lat = jnp.einsum('bhqk,bkr->bqhr', pr, CKV[:, :kend])
        return jnp.einsum('bqhr,rhv->bqhv', o_lat, p["mla_w_uv"])

    o = sweep_blocks(block, S, lambda q1: min(n_k, -(-(off + q1) // CHUNK) * CHUNK))
    return o.reshape(B, S, H * MLA_D_V) @ p["mla_w_out"], (ckv, kr)


def swa_mixer(h, p, cache):
    B, S, _ = h.shape
    H, KV, G, hd = SWA_HEADS, SWA_KV_HEADS, SWA_GROUPS, HEAD_DIM
    proj = h @ p["swa_w_in"]
    q, k, v = jnp.split(proj, [H * hd, (H + KV) * hd], axis=-1)
    q = q.reshape(B, S, KV, G, hd)
    k, v = k.reshape(B, S, KV, hd), v.reshape(B, S, KV, hd)
    if cache is None:
        nC = S // CHUNK
        pad = ((0, 0), (WINDOW, 0), (0, 0), (0, 0))
        kc = jnp.pad(k, pad).reshape(B, nC + SWA_BAND_CHUNKS, CHUNK, KV, hd)
        vc = jnp.pad(v, pad).reshape(B, nC + SWA_BAND_CHUNKS, CHUNK, KV, hd)
        kb = jnp.concatenate([kc[:, j:j + nC] for j in range(SWA_BAND_CHUNKS + 1)], axis=2)
        vb = jnp.concatenate([vc[:, j:j + nC] for j in range(SWA_BAND_CHUNKS + 1)], axis=2)
        qb = q.reshape(B, nC, CHUNK, KV, G, hd)
        qpos = jnp.arange(S).reshape(nC, CHUNK)
        kpc = (jnp.arange(S + WINDOW) - WINDOW).reshape(nC + SWA_BAND_CHUNKS, CHUNK)
        kpos = jnp.concatenate([kpc[j:j + nC] for j in range(SWA_BAND_CHUNKS + 1)], axis=1)
        valid = kpos >= 0
        new_k, new_v = k[:, -WINDOW:], v[:, -WINDOW:]
    else:
        ck, cv = cache
        K, V = with_past(ck, k), with_past(cv, v)
        kb, vb, qb = K[:, None], V[:, None], q[:, None]
        qpos = jnp.arange(S)[None]
        kpos = jnp.arange(-ck.shape[1], S)[None]
        valid = jnp.ones(kpos.shape, dtype=bool)
        new_k, new_v = K[:, -WINDOW:], V[:, -WINDOW:]
    slopes = alibi_slopes(H).reshape(KV, G)
    s = jnp.einsum('bnqkgd,bnjkd->bnkgqj', qb, kb).astype(F32) * hd ** -0.5
    dist = jnp.abs(qpos[:, :, None] - kpos[:, None, :]).astype(F32)
    s = s - slopes[None, None, :, :, None, None] * dist[None, :, None, None]
    s = jnp.where(valid[None, :, None, None, None, :], s, NEG)
    sink = p["swa_sinks"].astype(F32).reshape(KV, G)[None, None, :, :, None, None]
    m = jnp.maximum(jnp.max(s, axis=-1, keepdims=True), sink)
    e = jnp.exp(s - m)
    pr = e / (jnp.sum(e, axis=-1, keepdims=True) + jnp.exp(sink - m))
    o = jnp.einsum('bnkgqj,bnjkd->bnqkgd', pr.astype(vb.dtype), vb).reshape(B, S, H * hd)
    return o @ p["swa_w_out"], (new_k, new_v)


def sb_mixer(h, p, cache):
    B, S, _ = h.shape
    H, hd = SB_HEADS, HEAD_DIM
    proj = h @ p["sb_w_in"]
    q, k, v = jnp.split(proj, [H * hd, 2 * H * hd], axis=-1)
    q, k, v = q.reshape(B, S, H, hd), k.reshape(B, S, H, hd), v.reshape(B, S, H, hd)
    ck, cv = cache if cache is not None else (None, None)
    off = 0 if cache is None else ck.shape[1]
    K, V = with_past(ck, k), with_past(cv, v)
    scale = hd ** -0.5

    def block(q0, q1, kend):
        qpos = off + jnp.arange(q0, q1)
        z = jnp.einsum('bqhd,bkhd->bhqk', q[:, q0:q1], K[:, :kend]).astype(F32) * scale
        mask = jnp.arange(kend)[None, :] < qpos[:, None]
        log_rem = jnp.where(mask, jax.nn.log_sigmoid(-z), 0.0)
        suffix = lax.cumsum(log_rem, axis=3, reverse=True) - log_rem
        a = jnp.where(mask, jnp.exp(jax.nn.log_sigmoid(z) + suffix), 0.0)
        return jnp.einsum('bhqk,bkhd->bqhd', a.astype(V.dtype), V[:, :kend])

    o = sweep_blocks(block, S, lambda q1: off + q1)
    return o.reshape(B, S, H * hd) @ p["sb_w_out"], (k, v)


MIXERS = (fox_mixer, mla_mixer, swa_mixer, sb_mixer)


def grouped_moe(h, router_w, router_b, w_gate, w_up, w_down):
    B, S, D = h.shape
    xt = h.reshape(B * S, D)
    scores = jax.nn.sigmoid((xt @ router_w).astype(F32))
    biased = scores + router_b.astype(F32)
    group_score = jnp.sum(lax.top_k(biased.reshape(-1, N_GROUPS, EXPERTS_PER_GROUP), TOP_K)[0], axis=-1)
    best = jnp.argmax(group_score, axis=-1)
    in_group = (jnp.arange(N_EXPERTS) // EXPERTS_PER_GROUP)[None, :] == best[:, None]
    _, idx = lax.top_k(jnp.where(in_group, biased, NEG), TOP_K)
    sel = jnp.take_along_axis(scores, idx, axis=-1)
    wts = sel / jnp.sum(sel, axis=-1, keepdims=True)
    gates = jnp.sum(jax.nn.one_hot(idx, N_EXPERTS, dtype=F32) * wts[..., None], axis=1)
    hg = jnp.einsum('nd,edf->nef', xt, w_gate)
    hu = jnp.einsum('nd,edf->nef', xt, w_up)
    act = (jax.nn.silu(hg) * hu * gates[..., None].astype(hu.dtype))
    return jnp.einsum('nef,efd->nd', act, w_down).reshape(B, S, D)


def trunk(x, c, caches, p):
    states = []
    cs = jax.nn.silu(c)
    for i in range(DEPTH):
        kind = i % N_MIXERS
        mod = (cs @ p["ada_w"][i] + p["ada_b"][i])[:, None, :]
        sh_a, sc_a, g_a, sh_f, sc_f, g_f = jnp.split(mod, 6, axis=-1)
        out, st = MIXERS[kind](x * (1 + sc_a) + sh_a, p, caches[kind])
        states.extend(st)
        x = layer_norm(DEEPNORM_ALPHA * x + (1 + g_a) * out, p["ln_g"][i, 0], p["ln_b"][i, 0])
        f = grouped_moe(x * (1 + sc_f) + sh_f, p["router_w"], p["router_b"],
                        p["moe_w_gate"][i], p["moe_w_up"][i], p["moe_w_down"][i])
        x = layer_norm(DEEPNORM_ALPHA * x + (1 + g_f) * f, p["ln_g"][i, 1], p["ln_b"][i, 1])
    return x, states


def setup_inputs(seed: int = 0) -> dict:
    key = jax.random.key(seed)
    ks = iter(jax.random.split(key, 64))

    def nrm(shape, scale=1.0):
        return jax.random.normal(next(ks), shape, F32) * scale

    inv = D_MODEL ** -0.5
    fox_cols = 3 * FOX_HEADS * HEAD_DIM + FOX_HEADS
    return {
        "x_prompt": nrm((BATCH, SEQ, D_MODEL)),
        "x_sample": nrm((DEC_BATCH, DEC_SEQ, D_MODEL)),
        "cache_fox_k": nrm((DEC_BATCH, PAST_LEN, FOX_HEADS, HEAD_DIM)),
        "cache_fox_v": nrm((DEC_BATCH, PAST_LEN, FOX_HEADS, HEAD_DIM)),
        "cache_fox_logf": jax.nn.log_sigmoid(3.5 + nrm((DEC_BATCH, PAST_LEN, FOX_HEADS))),
        "cache_mla_ckv": nrm((DEC_BATCH, PAST_LEN, MLA_KV_RANK)),
        "cache_mla_krope": nrm((DEC_BATCH, PAST_LEN, MLA_D_ROPE)),
        "cache_swa_k": nrm((DEC_BATCH, WINDOW, SWA_KV_HEADS, HEAD_DIM)),
        "cache_swa_v": nrm((DEC_BATCH, WINDOW, SWA_KV_HEADS, HEAD_DIM)),
        "cache_sb_k": nrm((DEC_BATCH, PAST_LEN, SB_HEADS, HEAD_DIM)),
        "cache_sb_v": nrm((DEC_BATCH, PAST_LEN, SB_HEADS, HEAD_DIM)),
        "c_prompt": nrm((BATCH, D_MODEL)),
        "c_sample": nrm((DEC_BATCH, D_MODEL)),
        "ada_w": nrm((DEPTH, D_MODEL, 6 * D_MODEL), 0.1 * inv),
        "ada_b": nrm((DEPTH, 6 * D_MODEL), 0.01),
        "ln_g": 1.0 + nrm((DEPTH, 2, D_MODEL), 0.02),
        "ln_b": nrm((DEPTH, 2, D_MODEL), 0.02),
        "fox_w_in": nrm((D_MODEL, fox_cols), inv),
        "fox_b_f": jnp.linspace(1.0, 6.0, FOX_HEADS, dtype=F32) + nrm((FOX_HEADS,), 0.1),
        "fox_w_out": nrm((FOX_HEADS * HEAD_DIM, D_MODEL), (FOX_HEADS * HEAD_DIM) ** -0.5 * DEEPNORM_BETA),
        "mla_w_in": nrm((D_MODEL, MLA_Q_RANK + MLA_KV_RANK + MLA_D_ROPE), inv),
        "mla_q_norm": 1.0 + nrm((MLA_Q_RANK,), 0.02),
        "mla_w_uq": nrm((MLA_Q_RANK, MLA_HEADS * (MLA_D_NOPE + MLA_D_ROPE)), MLA_Q_RANK ** -0.5),
        "mla_kv_norm": 1.0 + nrm((MLA_KV_RANK,), 0.02),
        "mla_w_uk": nrm((MLA_KV_RANK, MLA_HEADS, MLA_D_NOPE), MLA_KV_RANK ** -0.5),
        "mla_w_uv": nrm((MLA_KV_RANK, MLA_HEADS, MLA_D_V), MLA_KV_RANK ** -0.5),
        "mla_w_out": nrm((MLA_HEADS * MLA_D_V, D_MODEL), (MLA_HEADS * MLA_D_V) ** -0.5 * DEEPNORM_BETA),
        "swa_w_in": nrm((D_MODEL, (SWA_HEADS + 2 * SWA_KV_HEADS) * HEAD_DIM), inv),
        "swa_sinks": nrm((SWA_HEADS,), 1.0),
        "swa_w_out": nrm((SWA_HEADS * HEAD_DIM, D_MODEL), (SWA_HEADS * HEAD_DIM) ** -0.5 * DEEPNORM_BETA),
        "sb_w_in": nrm((D_MODEL, 3 * SB_HEADS * HEAD_DIM), inv),
        "sb_w_out": nrm((SB_HEADS * HEAD_DIM, D_MODEL), (SB_HEADS * HEAD_DIM) ** -0.5 * DEEPNORM_BETA),
        "router_w": nrm((D_MODEL, N_EXPERTS), inv),
        "router_b": nrm((N_EXPERTS,), 0.01),
        "moe_w_gate": nrm((DEPTH, N_EXPERTS, D_MODEL, D_EXPERT), inv),
        "moe_w_up": nrm((DEPTH, N_EXPERTS, D_MODEL, D_EXPERT), inv),
        "moe_w_down": nrm((DEPTH, N_EXPERTS, D_EXPERT, D_MODEL), D_EXPERT ** -0.5 * DEEPNORM_BETA),
    }


def reference(x_prompt, x_sample, cache_fox_k, cache_fox_v, cache_fox_logf, cache_mla_ckv, cache_mla_krope,
              cache_swa_k, cache_swa_v, cache_sb_k, cache_sb_v, c_prompt, c_sample,
              ada_w, ada_b, ln_g, ln_b, fox_w_in, fox_b_f, fox_w_out,
              mla_w_in, mla_q_norm, mla_w_uq, mla_kv_norm, mla_w_uk, mla_w_uv, mla_w_out,
              swa_w_in, swa_sinks, swa_w_out, sb_w_in, sb_w_out,
              router_w, router_b, moe_w_gate, moe_w_up, moe_w_down):
    p = dict(ada_w=ada_w, ada_b=ada_b, ln_g=ln_g, ln_b=ln_b,
             fox_w_in=fox_w_in, fox_b_f=fox_b_f, fox_w_out=fox_w_out,
             mla_w_in=mla_w_in, mla_q_norm=mla_q_norm, mla_w_uq=mla_w_uq, mla_kv_norm=mla_kv_norm,
             mla_w_uk=mla_w_uk, mla_w_uv=mla_w_uv, mla_w_out=mla_w_out,
             swa_w_in=swa_w_in, swa_sinks=swa_sinks, swa_w_out=swa_w_out,
             sb_w_in=sb_w_in, sb_w_out=sb_w_out,
             router_w=router_w, router_b=router_b,
             moe_w_gate=moe_w_gate, moe_w_up=moe_w_up, moe_w_down=moe_w_down)
    y_prompt, st_p = trunk(x_prompt, c_prompt, (None, None, None, None), p)
    sample_caches = ((cache_fox_k, cache_fox_v, cache_fox_logf), (cache_mla_ckv, cache_mla_krope),
                     (cache_swa_k, cache_swa_v), (cache_sb_k, cache_sb_v))
    y_sample, st_s = trunk(x_sample, c_sample, sample_caches, p)
    fox_k_p, fox_v_p, fox_logf_p, mla_ckv_p, mla_krope_p, swa_k_p, swa_v_p, sb_k_p, sb_v_p = st_p
    fox_k_s, fox_v_s, fox_logf_s, mla_ckv_s, mla_krope_s, swa_k_s, swa_v_s, sb_k_s, sb_v_s = st_s
    return (y_prompt, y_sample,
            fox_k_p, fox_k_s, fox_v_p, fox_v_s, fox_logf_p, fox_logf_s,
            mla_ckv_p, mla_ckv_s, mla_krope_p, mla_krope_s,
            swa_k_p, swa_k_s, swa_v_p, swa_v_s,
            sb_k_p, sb_k_s, sb_v_p, sb_v_s)
```

```python
import functools
import math

import jax
import jax.numpy as jnp
from jax import lax
from jax.experimental import pallas as pl
from jax.experimental.pallas import tpu as pltpu

F32 = jnp.float32
BF16 = jnp.bfloat16

D_MODEL = 1024
DEPTH = 4
CHUNK = 64
HEAD_DIM = 64
N_HEADS = 16
N_PAIRS = N_HEADS // 2
LANES = 128
MLA_Q_RANK = 384
MLA_KV_RANK = 256
MLA_D_NOPE = 64
MLA_D_ROPE = 32
ROPE_THETA = 10000.0
SWA_KV_HEADS = 4
SWA_GROUPS = 4
WINDOW = 128
N_EXPERTS = 16
D_EXPERT = 256
DEEPNORM_ALPHA = (2.0 * DEPTH) ** 0.25
NEG = -1e30
VMEM_LIMIT = 56 * 1024 * 1024


def _cparams(*sem):
    return pltpu.CompilerParams(dimension_semantics=sem, vmem_limit_bytes=VMEM_LIMIT)


def _dot(a, b):
    return jnp.dot(a, b, preferred_element_type=F32)


def _dot_nt(a, b):
    return lax.dot_general(a, b, (((1,), (1,)), ((), ())), preferred_element_type=F32)


def _dot_f32(a, b):
    return lax.dot_general(a, b, (((1,), (0,)), ((), ())), precision=lax.Precision.HIGHEST,
                           preferred_element_type=F32)


def _sigmoid(x):
    return 1.0 / (1.0 + jnp.exp(-x))


def _log_sigmoid(x):
    return jnp.minimum(x, 0.0) - jnp.log1p(jnp.exp(-jnp.abs(x)))


def _layer_norm(y, g, b):
    mu = jnp.mean(y, axis=-1, keepdims=True)
    yc = y - mu
    var = jnp.mean(yc * yc, axis=-1, keepdims=True)
    return yc * lax.rsqrt(var + 1e-5) * g + b


def _ada_kernel(c_ref, w_ref, b_ref, o_ref):
    c = c_ref[...]
    o_ref[0] = _dot_f32(c * _sigmoid(c), w_ref[0]) + b_ref[0]


def _ada_mod(c, ada_w, ada_b):
    n, d = c.shape
    depth, _, n6 = ada_w.shape
    tn = 1536
    return pl.pallas_call(
        _ada_kernel,
        grid=(depth, n6 // tn),
        in_specs=[pl.BlockSpec((n, d), lambda i, j: (0, 0)),
                  pl.BlockSpec((1, d, tn), lambda i, j: (i, 0, j)),
                  pl.BlockSpec((1, 1, tn), lambda i, j: (i, 0, j))],
        out_specs=pl.BlockSpec((1, n, tn), lambda i, j: (i, 0, j)),
        out_shape=jax.ShapeDtypeStruct((depth, n, n6), F32),
        compiler_params=_cparams("parallel", "parallel"),
        name="ada_mod",
    )(c, ada_w, ada_b.reshape(depth, 1, n6))


def _proj_kernel(*refs, n_w, emits, has_bias):
    x_ref, sc_ref, sh_ref = refs[:3]
    w_refs = refs[3:3 + n_w]
    pos = 3 + n_w
    bias_ref = refs[pos] if has_bias else None
    o_refs = refs[pos + (1 if has_bias else 0):]
    h = (x_ref[0] * (1.0 + sc_ref[0]) + sh_ref[0]).astype(BF16)
    oi = 0
    for j in range(n_w):
        y = _dot(h, w_refs[j][...])
        for kind in emits[j]:
            if kind == "logf":
                o_refs[oi][0] = _log_sigmoid(y + bias_ref[...])[:, :N_HEADS]
            else:
                o_refs[oi][0] = y.astype(kind)
            oi += 1


def _row_tile(s, target):
    return min(s, target)


def _proj(x, sc, sh, weights, emits, bias=None, tm=512, name="proj"):
    b, s, d = x.shape
    tm = _row_tile(s, tm)
    in_specs = [pl.BlockSpec((1, tm, d), lambda i, j: (i, j, 0)),
                pl.BlockSpec((1, 1, d), lambda i, j: (i, 0, 0)),
                pl.BlockSpec((1, 1, d), lambda i, j: (i, 0, 0))]
    args = [x, sc, sh]
    for w in weights:
        in_specs.append(pl.BlockSpec(w.shape, lambda i, j: (0, 0)))
        args.append(w)
    if bias is not None:
        in_specs.append(pl.BlockSpec(bias.shape, lambda i, j: (0, 0)))
        args.append(bias)
    out_specs, out_shape = [], []
    for w, em in zip(weights, emits):
        for kind in em:
            n, dt = (N_HEADS, F32) if kind == "logf" else (w.shape[1], kind)
            out_specs.append(pl.BlockSpec((1, tm, n), lambda i, j: (i, j, 0)))
            out_shape.append(jax.ShapeDtypeStruct((b, s, n), dt))
    return pl.pallas_call(
        functools.partial(_proj_kernel, n_w=len(weights), emits=emits, has_bias=bias is not None),
        grid=(b, s // tm),
        in_specs=in_specs, out_specs=out_specs, out_shape=out_shape,
        compiler_params=_cparams("parallel", "parallel"),
        name=name,
    )(*args)


def _cumsum_kernel(x_ref, f_ref, ft_ref, pad_ref):
    nblk = x_ref.shape[1] // LANES
    r = lax.broadcasted_iota(jnp.int32, (LANES, LANES), 0)
    c = lax.broadcasted_iota(jnp.int32, (LANES, LANES), 1)
    tri = (c <= r).astype(F32)
    pad_ref[...] = jnp.zeros_like(pad_ref)

    def body(i, carry):
        r0 = pl.multiple_of(i * LANES, LANES)
        pad_ref[:, :N_HEADS] = x_ref[0, pl.ds(r0, LANES), :]
        f = _dot_f32(tri, pad_ref[...]) + carry
        f_ref[0, pl.ds(r0, LANES), :] = f[:, :N_HEADS]
        ft_ref[0, :, pl.ds(r0, LANES)] = f.T[:N_HEADS, :]
        return f[LANES - 1:LANES, :]

    lax.fori_loop(0, nblk, body, jnp.zeros((1, LANES), F32))


def _forget_cumsum(logf):
    b, s, h = logf.shape
    return pl.pallas_call(
        _cumsum_kernel,
        grid=(b,),
        in_specs=[pl.BlockSpec((1, s, h), lambda i: (i, 0, 0))],
        out_specs=[pl.BlockSpec((1, s, h), lambda i: (i, 0, 0)),
                   pl.BlockSpec((1, h, s), lambda i: (i, 0, 0))],
        out_shape=[jax.ShapeDtypeStruct((b, s, h), F32), jax.ShapeDtypeStruct((b, h, s), F32)],
        scratch_shapes=[pltpu.VMEM((LANES, LANES), F32)],
        compiler_params=_cparams("parallel"),
        name="forget_cumsum",
    )(logf)


def _half_mask(shape):
    return lax.broadcasted_iota(jnp.int32, shape, 1) < HEAD_DIM


def _flash_kernel(*refs, kind, sq, off, tq, tk):
    if kind == "fox":
        q_ref, k_ref, v_ref, fq_ref, ft_ref, o_ref = refs
    else:
        q_ref, k_ref, v_ref, o_ref = refs
    nq = sq // tq
    row = lax.broadcasted_iota(jnp.int32, (tq, tk), 0)
    col = lax.broadcasted_iota(jnp.int32, (tq, tk), 1)
    lo_half = _half_mask((tq, LANES))

    def qblock(qi, _):
        q0 = pl.multiple_of(qi * tq, tq)
        qblk = q_ref[0, pl.ds(q0, tq), :]
        if kind == "fox":
            n_full = (off + q0 + 1) // tk
            n_total = (off + q0 + tq + tk - 1) // tk
        else:
            n_full = ((off + q0) // CHUNK * CHUNK + CHUNK) // tk
            n_total = (((off + q0 + tq - 1) // CHUNK + 1) * CHUNK + tk - 1) // tk
        outs = []
        for hh in range(2):
            if kind == "fox":
                qh = jnp.where(lo_half if hh == 0 else ~lo_half, qblk, jnp.zeros_like(qblk))
                fq = fq_ref[0, 0, pl.ds(q0, tq), hh:hh + 1]
            else:
                qh = qblk[:, hh * LANES:(hh + 1) * LANES]

            def kstep(j, carry, masked):
                m, l, acc = carry
                k0 = pl.multiple_of(j * tk, tk)
                if kind == "fox":
                    kb = k_ref[0, pl.ds(k0, tk), :]
                else:
                    kb = k_ref[0, pl.ds(k0, tk), hh * LANES:(hh + 1) * LANES]
                s = _dot_nt(qh, kb)
                if kind == "fox":
                    s = s + fq - ft_ref[0, 0, hh:hh + 1, pl.ds(k0, tk)]
                if masked:
                    qpos = off + q0 + row
                    kpos = k0 + col
                    if kind == "fox":
                        vis = kpos <= qpos
                    else:
                        vis = (kpos // CHUNK) <= (qpos // CHUNK)
                    s = jnp.where(vis, s, NEG)
                m_new = jnp.maximum(m, jnp.max(s, axis=-1, keepdims=True))
                p = jnp.exp(s - m_new)
                alpha = jnp.exp(m - m_new)
                l_new = alpha * l + jnp.sum(p, axis=-1, keepdims=True)
                acc_new = alpha * acc + _dot(p.astype(BF16), v_ref[0, pl.ds(k0, tk), :])
                return m_new, l_new, acc_new

            carry = (jnp.full((tq, 1), NEG, F32), jnp.zeros((tq, 1), F32),
                     jnp.zeros((tq, LANES), F32))
            carry = lax.fori_loop(0, n_full, functools.partial(kstep, masked=False), carry)
            carry = lax.fori_loop(n_full, n_total, functools.partial(kstep, masked=True), carry)
            outs.append(carry[2] / carry[1])
        o_ref[0, pl.ds(q0, tq), :] = jnp.where(lo_half, outs[0], outs[1]).astype(o_ref.dtype)
        return 0

    lax.fori_loop(0, nq, qblock, 0)


def _flash(kind, q, k, v, off, tq, tk, fq=None, ft=None):
    b, sq, qw = q.shape
    skp = k.shape[1]
    ql = qw // N_PAIRS
    in_specs = [pl.BlockSpec((1, sq, ql), lambda i, p: (i, 0, p)),
                pl.BlockSpec((1, skp, ql), lambda i, p: (i, 0, p)),
                pl.BlockSpec((1, skp, LANES), lambda i, p: (i, 0, p))]
    args = [q, k, v]
    if kind == "fox":
        in_specs += [pl.BlockSpec((1, 1, sq, 2), lambda i, p: (i, p, 0, 0)),
                     pl.BlockSpec((1, 1, 2, skp), lambda i, p: (i, p, 0, 0))]
        args += [fq, ft]
    return pl.pallas_call(
        functools.partial(_flash_kernel, kind=kind, sq=sq, off=off, tq=tq, tk=tk),
        grid=(b, N_PAIRS),
        in_specs=in_specs,
        out_specs=pl.BlockSpec((1, sq, LANES), lambda i, p: (i, 0, p)),
        out_shape=jax.ShapeDtypeStruct((b, sq, N_HEADS * HEAD_DIM), BF16),
        compiler_params=_cparams("parallel", "parallel"),
        name="attn_" + kind,
    )(*args)


def _sb_kernel(q_ref, k_ref, v_ref, o_ref, *, sq, off, tq, tk):
    nq = sq // tq
    row = lax.broadcasted_iota(jnp.int32, (tq, tk), 0)
    col = lax.broadcasted_iota(jnp.int32, (tq, tk), 1)
    lo_half = _half_mask((tq, LANES))
    kr = lax.broadcasted_iota(jnp.int32, (tk, tk), 0)
    kc = lax.broadcasted_iota(jnp.int32, (tk, tk), 1)
    later = jnp.where(kr > kc, 1.0, 0.0).astype(BF16)

    def qblock(qi, _):
        q0 = pl.multiple_of(qi * tq, tq)
        qblk = q_ref[0, pl.ds(q0, tq), :]
        n_full = (off + q0) // tk
        n_total = (off + q0 + tq - 1 + tk - 1) // tk
        outs = []
        for hh in range(2):
            qh = jnp.where(lo_half if hh == 0 else ~lo_half, qblk, jnp.zeros_like(qblk))

            def kstep(jj, carry, masked, base):
                rem, acc = carry
                j = base - jj
                k0 = pl.multiple_of(j * tk, tk)
                z = _dot_nt(qh, k_ref[0, pl.ds(k0, tk), :])
                lsm = jnp.minimum(-z, 0.0) - jnp.log1p(jnp.exp(-jnp.abs(z)))
                if masked:
                    vis = (k0 + col) < (off + q0 + row)
                    lr = jnp.where(vis, lsm, 0.0)
                else:
                    lr = lsm
                hi = lr.astype(BF16)
                lo = (lr - hi.astype(F32)).astype(BF16)
                suf = _dot(hi, later) + _dot(lo, later)
                a = jnp.exp(z + lsm + suf + rem)
                if masked:
                    a = jnp.where(vis, a, 0.0)
                acc_new = acc + _dot(a.astype(BF16), v_ref[0, pl.ds(k0, tk), :])
                rem_new = rem + suf[:, 0:1] + lr[:, 0:1]
                return rem_new, acc_new

            carry = (jnp.zeros((tq, 1), F32), jnp.zeros((tq, LANES), F32))
            carry = lax.fori_loop(0, n_total - n_full,
                                  functools.partial(kstep, masked=True, base=n_total - 1), carry)
            carry = lax.fori_loop(0, n_full,
                                  functools.partial(kstep, masked=False, base=n_full - 1), carry)
            outs.append(carry[1])
        o_ref[0, pl.ds(q0, tq), :] = jnp.where(lo_half, outs[0], outs[1]).astype(o_ref.dtype)
        return 0

    lax.fori_loop(0, nq, qblock, 0)


def _sb_attn(q, k, v, off, tq, tk):
    b, sq, _ = q.shape
    skp = k.shape[1]
    return pl.pallas_call(
        functools.partial(_sb_kernel, sq=sq, off=off, tq=tq, tk=tk),
        grid=(b, N_PAIRS),
        in_specs=[pl.BlockSpec((1, sq, LANES), lambda i, p: (i, 0, p)),
                  pl.BlockSpec((1, skp, LANES), lambda i, p: (i, 0, p)),
                  pl.BlockSpec((1, skp, LANES), lambda i, p: (i, 0, p))],
        out_specs=pl.BlockSpec((1, sq, LANES), lambda i, p: (i, 0, p)),
        out_shape=jax.ShapeDtypeStruct((b, sq, N_HEADS * HEAD_DIM), BF16),
        compiler_params=_cparams("parallel", "parallel"),
        name="attn_sb",
    )(q, k, v)


SWA_SPAN = 2 * WINDOW


def _swa_kernel(slope_ref, sink_ref, q_ref, k_ref, v_ref, o_ref, *, sq, past, tq):
    nq = sq // tq
    kv = pl.program_id(1)
    row = lax.broadcasted_iota(jnp.int32, (tq, SWA_SPAN), 0)
    col = lax.broadcasted_iota(jnp.int32, (tq, SWA_SPAN), 1)
    lo_half = _half_mask((tq, LANES))

    def qblock(qi, _):
        q0 = pl.multiple_of(qi * tq, tq)
        ks = pl.multiple_of(jnp.maximum(q0 + past - WINDOW, 0), CHUNK)
        kwin = k_ref[0, pl.ds(ks, SWA_SPAN), :]
        vwin = v_ref[0, pl.ds(ks, SWA_SPAN), :]
        qpos = q0 + row
        kpos = ks - past + col
        qc = qpos // CHUNK
        vis = (kpos >= qc * CHUNK - WINDOW) & (kpos < (qc + 1) * CHUNK)
        dist = jnp.abs(qpos - kpos).astype(F32)
        for pair in range(SWA_GROUPS // 2):
            qpair = q_ref[0, pl.ds(q0, tq), pair * LANES:(pair + 1) * LANES]
            outs = []
            for half in range(2):
                head = kv * SWA_GROUPS + pair * 2 + half
                qh = jnp.where(lo_half if half == 0 else ~lo_half, qpair, jnp.zeros_like(qpair))
                s = _dot_nt(qh, kwin) - slope_ref[head] * dist
                s = jnp.where(vis, s, NEG)
                sink = sink_ref[head]
                m = jnp.maximum(jnp.max(s, axis=-1, keepdims=True), sink)
                e = jnp.exp(s - m)
                den = jnp.sum(e, axis=-1, keepdims=True) + jnp.exp(sink - m)
                outs.append(_dot((e / den).astype(BF16), vwin))
            o_ref[0, pl.ds(q0, tq), pair * LANES:(pair + 1) * LANES] = (
                jnp.where(lo_half, outs[0], outs[1]).astype(o_ref.dtype))
        return 0

    lax.fori_loop(0, nq, qblock, 0)


def _swa_attn(q, kdup, vdup, slopes, sinks, past, tq):
    b, sq, _ = q.shape
    sk = kdup.shape[1]
    gw = SWA_GROUPS * HEAD_DIM
    smem = pl.BlockSpec(memory_space=pltpu.SMEM)
    return pl.pallas_call(
        functools.partial(_swa_kernel, sq=sq, past=past, tq=tq),
        grid=(b, SWA_KV_HEADS),
        in_specs=[smem, smem,
                  pl.BlockSpec((1, sq, gw), lambda i, p: (i, 0, p)),
                  pl.BlockSpec((1, sk, LANES), lambda i, p: (i, 0, p)),
                  pl.BlockSpec((1, sk, LANES), lambda i, p: (i, 0, p))],
        out_specs=pl.BlockSpec((1, sq, gw), lambda i, p: (i, 0, p)),
        out_shape=jax.ShapeDtypeStruct((b, sq, N_HEADS * HEAD_DIM), BF16),
        compiler_params=_cparams("parallel", "parallel"),
        name="attn_swa",
    )(slopes, sinks, q, kdup, vdup)


MLA_IN_COLS = MLA_Q_RANK + MLA_KV_RANK + 2 * LANES
MLA_HEAD_W = LANES


def _rms_norm(x, g):
    return x * lax.rsqrt(jnp.mean(x * x, axis=-1, keepdims=True) + 1e-6) * g


def _mla_proj_kernel(x_ref, sc_ref, sh_ref, win_ref, gq_ref, gkv_ref, wqa_ref, wqb_ref,
                     wk_ref, wv_ref, qc_ref, qs_ref, kc_ref, ks_ref,
                     q_ref, kf_ref, v_ref, ckv_ref, kr_ref):
    h = (x_ref[0] * (1.0 + sc_ref[0]) + sh_ref[0]).astype(BF16)
    proj = _dot(h, win_ref[...])
    cq = _rms_norm(proj[:, :MLA_Q_RANK], gq_ref[...]).astype(BF16)
    ckv = _rms_norm(proj[:, MLA_Q_RANK:MLA_Q_RANK + MLA_KV_RANK], gkv_ref[...])
    ckv_ref[0] = ckv
    ckv_b = ckv.astype(BF16)
    kr0 = MLA_Q_RANK + MLA_KV_RANK
    krr = proj[:, kr0:kr0 + LANES] * kc_ref[...] + proj[:, kr0 + LANES:kr0 + 2 * LANES] * ks_ref[...]
    kr_ref[0] = pltpu.roll(krr, LANES - MLA_D_NOPE, 1)[:, :MLA_D_ROPE]
    qa = _dot(cq, wqa_ref[...])
    qb = _dot(cq, wqb_ref[...])
    kn = _dot(ckv_b, wk_ref[...])
    v_ref[0] = _dot(ckv_b, wv_ref[...]).astype(BF16)
    qc, qs = qc_ref[...], qs_ref[...]
    for hd in range(N_HEADS):
        sl = slice(hd * MLA_HEAD_W, (hd + 1) * MLA_HEAD_W)
        q_ref[0, :, sl] = (qa[:, sl] * qc + qb[:, sl] * qs).astype(BF16)
        kf_ref[0, :, sl] = (kn[:, sl] + krr).astype(BF16)


def _mla_proj(x, sc, sh, mw, tables, tm=256):
    b, s, d = x.shape
    tm = _row_tile(s, tm)
    qc, qs, kc, ks = tables
    full = lambda a: pl.BlockSpec(a.shape, lambda i, j: (0,) * a.ndim)
    tab = pl.BlockSpec((tm, LANES), lambda i, j: (j, 0))
    row = lambda n: pl.BlockSpec((1, tm, n), lambda i, j: (i, j, 0))
    hw = N_HEADS * MLA_HEAD_W
    return pl.pallas_call(
        _mla_proj_kernel,
        grid=(b, s // tm),
        in_specs=[row(d),
                  pl.BlockSpec((1, 1, d), lambda i, j: (i, 0, 0)),
                  pl.BlockSpec((1, 1, d), lambda i, j: (i, 0, 0)),
                  full(mw["win"]), full(mw["gq"]), full(mw["gkv"]), full(mw["wqa"]),
                  full(mw["wqb"]), full(mw["wk"]), full(mw["wv"]), tab, tab, tab, tab],
        out_specs=[row(hw), row(hw), row(N_HEADS * HEAD_DIM), row(MLA_KV_RANK), row(MLA_D_ROPE)],
        out_shape=[jax.ShapeDtypeStruct((b, s, hw), BF16),
                   jax.ShapeDtypeStruct((b, s, hw), BF16),
                   jax.ShapeDtypeStruct((b, s, N_HEADS * HEAD_DIM), BF16),
                   jax.ShapeDtypeStruct((b, s, MLA_KV_RANK), F32),
                   jax.ShapeDtypeStruct((b, s, MLA_D_ROPE), F32)],
        compiler_params=_cparams("parallel", "parallel"),
        name="proj_mla",
    )(x, sc, sh, mw["win"], mw["gq"], mw["gkv"], mw["wqa"], mw["wqb"], mw["wk"], mw["wv"],
      qc, qs, kc, ks)


def _mla_cache_kernel(ckv_ref, kr_ref, wk_ref, wv_ref, place_ref, kf_ref, v_ref):
    ckv_b = ckv_ref[0].astype(BF16)
    kf_ref[0] = (_dot(ckv_b, wk_ref[...]) + _dot(kr_ref[0], place_ref[...])).astype(BF16)
    v_ref[0] = _dot(ckv_b, wv_ref[...]).astype(BF16)


def _mla_cache_up(ckv, kr_pad, mw, tm=512):
    b, p, r = ckv.shape
    tm = _row_tile(p, tm)
    hw = N_HEADS * MLA_HEAD_W
    full = lambda a: pl.BlockSpec(a.shape, lambda i, j: (0,) * a.ndim)
    row = lambda n: pl.BlockSpec((1, tm, n), lambda i, j: (i, j, 0))
    return pl.pallas_call(
        _mla_cache_kernel,
        grid=(b, p // tm),
        in_specs=[row(r), row(LANES), full(mw["wk"]), full(mw["wv"]), full(mw["place"])],
        out_specs=[row(hw), row(N_HEADS * HEAD_DIM)],
        out_shape=[jax.ShapeDtypeStruct((b, p, hw), BF16),
                   jax.ShapeDtypeStruct((b, p, N_HEADS * HEAD_DIM), BF16)],
        compiler_params=_cparams("parallel", "parallel"),
        name="mla_cache_up",
    )(ckv, kr_pad, mw["wk"], mw["wv"], mw["place"])


def _xor_partner(x, k, lane):
    up = pltpu.roll(x, LANES - k, 1)
    dn = pltpu.roll(x, k, 1)
    return jnp.where((lane & k) == 0, up, dn)


def _lane_argmax(v, idx, lane, strides):
    for k in strides:
        pv = _xor_partner(v, k, lane)
        pi = _xor_partner(idx, k, lane)
        take = (pv > v) | ((pv == v) & (pi < idx))
        v = jnp.where(take, pv, v)
        idx = jnp.where(take, pi, idx)
    return v, idx


def _route(logits, rb):
    lane = lax.broadcasted_iota(jnp.int32, logits.shape, 1)
    scores = _sigmoid(logits)
    biased = scores + rb
    p1 = _xor_partner(biased, 1, lane)
    hi1, lo1 = jnp.maximum(biased, p1), jnp.minimum(biased, p1)
    hi2, lo2 = _xor_partner(hi1, 2, lane), _xor_partner(lo1, 2, lane)
    group_score = jnp.maximum(hi1, hi2) + jnp.maximum(jnp.minimum(hi1, hi2), jnp.maximum(lo1, lo2))
    gid = lane >> 2
    _, best = _lane_argmax(group_score, gid, lane, (4, 8))
    cand = jnp.where(gid == best, biased, NEG)
    _, i1 = _lane_argmax(cand, lane, lane, (1, 2, 4, 8))
    _, i2 = _lane_argmax(jnp.where(lane == i1, -jnp.inf, cand), lane, lane, (1, 2, 4, 8))
    sel = jnp.where((lane < N_EXPERTS) & ((lane == i1) | (lane == i2)), scores, 0.0)
    return sel / jnp.sum(sel, axis=-1, keepdims=True)


def _outproj_kernel(o_ref, x_ref, w_ref, ga_ref, scf_ref, shf_ref, g_ref, b_ref,
                    rwh_ref, rwl_ref, rb_ref, x1_ref, h2_ref, gate_ref):
    y = DEEPNORM_ALPHA * x_ref[0] + (1.0 + ga_ref[0]) * _dot(o_ref[0], w_ref[...])
    x1 = _layer_norm(y, g_ref[...], b_ref[...])
    x1_ref[0] = x1
    h2 = x1 * (1.0 + scf_ref[0]) + shf_ref[0]
    hi = h2.astype(BF16)
    h2_ref[0] = hi
    lo = (h2 - hi.astype(F32)).astype(BF16)
    logits = _dot(hi, rwh_ref[...]) + _dot(lo, rwh_ref[...]) + _dot(hi, rwl_ref[...])
    gate_ref[0] = _route(logits, rb_ref[...])[:, :N_EXPERTS]


def _outproj(o, x, w, ga, scf, shf, g, bta, rwh, rwl, rb, tm=512):
    b, s, d = x.shape
    tm = _row_tile(s, tm)
    row = lambda n: pl.BlockSpec((1, tm, n), lambda i, j: (i, j, 0))
    mod = pl.BlockSpec((1, 1, d), lambda i, j: (i, 0, 0))
    full = lambda a: pl.BlockSpec(a.shape, lambda i, j: (0,) * a.ndim)
    return pl.pallas_call(
        _outproj_kernel,
        grid=(b, s // tm),
        in_specs=[row(d), row(d), full(w), mod, mod, mod, full(g), full(bta),
                  full(rwh), full(rwl), full(rb)],
        out_specs=[row(d), row(d), row(N_EXPERTS)],
        out_shape=[jax.ShapeDtypeStruct((b, s, d), F32), jax.ShapeDtypeStruct((b, s, d), BF16),
                   jax.ShapeDtypeStruct((b, s, N_EXPERTS), F32)],
        compiler_params=_cparams("parallel", "parallel"),
        name="outproj_ln_router",
    )(o, x, w, ga, scf, shf, g, bta, rwh, rwl, rb)


def _moe_kernel(h_ref, gate_ref, x_ref, gf_ref, g_ref, b_ref, wg_ref, wu_ref, wd_ref,
                o_ref, acc_ref):
    e = pl.program_id(2)

    @pl.when(e == 0)
    def _():
        acc_ref[...] = jnp.zeros_like(acc_ref)

    h = h_ref[0]
    hg = _dot(h, wg_ref[0])
    hu = _dot(h, wu_ref[0])
    gates = gate_ref[0]
    lane = lax.broadcasted_iota(jnp.int32, gates.shape, 1)
    ge = jnp.sum(jnp.where(lane == e, gates, 0.0), axis=-1, keepdims=True)
    act = hg * _sigmoid(hg) * hu * ge
    acc_ref[...] += _dot(act.astype(BF16), wd_ref[0])

    @pl.when(e == pl.num_programs(2) - 1)
    def _():
        y = DEEPNORM_ALPHA * x_ref[0] + (1.0 + gf_ref[0]) * acc_ref[...]
        o_ref[0] = _layer_norm(y, g_ref[...], b_ref[...])


def _moe(h2, gates, x1, gf, g, bta, wg, wu, wd, tm=512):
    b, s, d = x1.shape
    tm = _row_tile(s, tm)
    ne, _, de = wg.shape
    row = lambda n: pl.BlockSpec((1, tm, n), lambda i, j, e: (i, j, 0))
    full = lambda a: pl.BlockSpec(a.shape, lambda i, j, e: (0,) * a.ndim)
    return pl.pallas_call(
        _moe_kernel,
        grid=(b, s // tm, ne),
        in_specs=[row(d), row(N_EXPERTS), row(d),
                  pl.BlockSpec((1, 1, d), lambda i, j, e: (i, 0, 0)), full(g), full(bta),
                  pl.BlockSpec((1, d, de), lambda i, j, e: (e, 0, 0)),
                  pl.BlockSpec((1, d, de), lambda i, j, e: (e, 0, 0)),
                  pl.BlockSpec((1, de, d), lambda i, j, e: (e, 0, 0))],
        out_specs=row(d),
        out_shape=jax.ShapeDtypeStruct((b, s, d), F32),
        scratch_shapes=[pltpu.VMEM((tm, d), F32)],
        compiler_params=_cparams("parallel", "parallel", "arbitrary"),
        name="moe_ffn",
    )(h2, gates, x1, gf, g, bta, wg, wu, wd)


def _pad_cols(w, n):
    return jnp.pad(w, ((0, 0), (0, n - w.shape[1])))


def _rot_half_cols(w):
    half = w.shape[-1] // 2
    return jnp.concatenate([-w[..., half:], w[..., :half]], axis=-1)


def _mla_weights(mla_w_in, mla_q_norm, mla_w_uq, mla_kv_norm, mla_w_uk, mla_w_uv):
    d = mla_w_in.shape[0]
    kr0 = MLA_Q_RANK + MLA_KV_RANK
    wkr = mla_w_in[:, kr0:]
    z = lambda n: jnp.zeros((d, n), F32)
    win = jnp.concatenate([mla_w_in[:, :kr0], z(MLA_D_NOPE), wkr, z(LANES - MLA_D_NOPE - MLA_D_ROPE),
                           z(MLA_D_NOPE), _rot_half_cols(wkr), z(LANES - MLA_D_NOPE - MLA_D_ROPE)],
                          axis=1)
    wuq = mla_w_uq.reshape(MLA_Q_RANK, N_HEADS, MLA_D_NOPE + MLA_D_ROPE)
    zq = lambda n: jnp.zeros((MLA_Q_RANK, N_HEADS, n), F32)
    tail = LANES - MLA_D_NOPE - MLA_D_ROPE
    wqa = jnp.concatenate([wuq, zq(tail)], axis=-1).reshape(MLA_Q_RANK, N_HEADS * LANES)
    wqb = jnp.concatenate([zq(MLA_D_NOPE), _rot_half_cols(wuq[..., MLA_D_NOPE:]), zq(tail)],
                          axis=-1).reshape(MLA_Q_RANK, N_HEADS * LANES)
    wk = jnp.concatenate([mla_w_uk, jnp.zeros((MLA_KV_RANK, N_HEADS, LANES - MLA_D_NOPE), F32)],
                         axis=-1).reshape(MLA_KV_RANK, N_HEADS * LANES)
    wv = mla_w_uv.reshape(MLA_KV_RANK, N_HEADS * HEAD_DIM)
    src = jnp.arange(LANES)[:, None]
    dst = jnp.arange(N_HEADS * LANES)[None, :]
    place = ((dst % LANES) == src + MLA_D_NOPE) & (src < MLA_D_ROPE)
    return dict(win=win.astype(BF16), gq=mla_q_norm.reshape(1, -1), gkv=mla_kv_norm.reshape(1, -1),
                wqa=wqa.astype(BF16), wqb=wqb.astype(BF16), wk=wk.astype(BF16),
                wv=wv.astype(BF16), place=place.astype(BF16))


def _rope_tables(n_pos):
    half = MLA_D_ROPE // 2
    inv = ROPE_THETA ** (-jnp.arange(half, dtype=F32) * 2.0 / MLA_D_ROPE)
    ang = jnp.arange(n_pos, dtype=F32)[:, None] * inv[None, :]
    cos = jnp.concatenate([jnp.cos(ang), jnp.cos(ang)], axis=-1)
    sin = jnp.concatenate([jnp.sin(ang), jnp.sin(ang)], axis=-1)
    scale = (MLA_D_NOPE + MLA_D_ROPE) ** -0.5
    z = lambda n: jnp.zeros((n_pos, n), F32)
    tail = LANES - MLA_D_NOPE - MLA_D_ROPE
    qc = jnp.concatenate([jnp.full((n_pos, MLA_D_NOPE), scale, F32), cos * scale, z(tail)], axis=1)
    qs = jnp.concatenate([z(MLA_D_NOPE), sin * scale, z(tail)], axis=1)
    kc = jnp.concatenate([z(MLA_D_NOPE), cos, z(tail)], axis=1)
    ks = jnp.concatenate([z(MLA_D_NOPE), sin, z(tail)], axis=1)
    return qc, qs, kc, ks


def _dup_heads(w):
    d = w.shape[0]
    w4 = w.reshape(d, SWA_KV_HEADS, HEAD_DIM)
    return jnp.concatenate([w4, w4], axis=-1).reshape(d, SWA_KV_HEADS * LANES)


def _pad_rows(a, n):
    return jnp.pad(a, ((0, 0), (0, n - a.shape[1]), (0, 0)))


def _round_up(n, m):
    return -(-n // m) * m


def _attn_tiles(sq):
    return (256, 256) if sq % 256 == 0 else (sq, 128)


def _fox_mixer(x, sc, sh, fw, cache):
    b, s, _ = x.shape
    q, k32, kb, v32, vb, logf = _proj(x, sc, sh, fw["w"], [[BF16], [F32, BF16], [F32, BF16], ["logf"]],
                                      bias=fw["bias"], name="proj_fox")
    tq, tk = _attn_tiles(s)
    if cache is None:
        off, kall, vall, lall = 0, kb, vb, logf
    else:
        ck, cv, clf = cache
        off = ck.shape[1]
        flat = lambda a: a.reshape(a.shape[0], a.shape[1], -1).astype(BF16)
        kall = jnp.concatenate([flat(ck), kb], axis=1)
        vall = jnp.concatenate([flat(cv), vb], axis=1)
        lall = jnp.concatenate([clf, logf], axis=1)
    skp = _round_up(off + s, max(tk, LANES))
    kall, vall, lall = _pad_rows(kall, skp), _pad_rows(vall, skp), _pad_rows(lall, skp)
    f, ft = _forget_cumsum(lall)
    fq = f[:, off:off + s].reshape(b, s, N_PAIRS, 2).transpose(0, 2, 1, 3)
    ft = ft.reshape(b, N_PAIRS, 2, skp)
    o = _flash("fox", q, kall, vall, off, tq, tk, fq=fq, ft=ft)
    hshape = (b, s, N_HEADS, HEAD_DIM)
    return o, (k32.reshape(hshape), v32.reshape(hshape), logf)


def _mla_mixer(x, sc, sh, mw, cache):
    b, s, _ = x.shape
    off = 0 if cache is None else cache[0].shape[1]
    tables = [t[off:off + s] for t in mw["tables"]]
    q, kf, v, ckv, kr = _mla_proj(x, sc, sh, mw, tables)
    tq, tk = _attn_tiles(s)
    if cache is not None:
        c_ckv, c_kr = cache
        kr_pad = jnp.pad(c_kr, ((0, 0), (0, 0), (0, LANES - MLA_D_ROPE))).astype(BF16)
        ckf, cvv = _mla_cache_up(c_ckv, kr_pad, mw)
        kf = jnp.concatenate([ckf, kf], axis=1)
        v = jnp.concatenate([cvv, v], axis=1)
    skp = _round_up(off + s, tk)
    kf, v = _pad_rows(kf, skp), _pad_rows(v, skp)
    o = _flash("mla", q, kf, v, off, tq, tk)
    return o, (ckv, kr)


def _swa_mixer(x, sc, sh, sw, cache):
    b, s, _ = x.shape
    q, k32, kdup, v32, vdup = _proj(x, sc, sh, sw["w"], [[BF16], [F32], [BF16], [F32], [BF16]],
                                    name="proj_swa")
    kvshape = lambda a: a.reshape(a.shape[0], a.shape[1], SWA_KV_HEADS, HEAD_DIM)
    if cache is None:
        past, tq = 0, 2 * CHUNK
        new_k, new_v = kvshape(k32[:, -WINDOW:]), kvshape(v32[:, -WINDOW:])
    else:
        ck, cv = cache
        past, tq = ck.shape[1], s
        dup = lambda c: jnp.concatenate([c, c], axis=-1).reshape(c.shape[0], c.shape[1], -1).astype(BF16)
        kdup = _pad_rows(jnp.concatenate([dup(ck), kdup], axis=1), SWA_SPAN)
        vdup = _pad_rows(jnp.concatenate([dup(cv), vdup], axis=1), SWA_SPAN)
        new_k = jnp.concatenate([ck, kvshape(k32)], axis=1)[:, -WINDOW:]
        new_v = jnp.concatenate([cv, kvshape(v32)], axis=1)[:, -WINDOW:]
    o = _swa_attn(q, kdup, vdup, sw["slopes"], sw["sinks"], past, tq)
    return o, (new_k, new_v)


def _sb_mixer(x, sc, sh, bw, cache):
    b, s, _ = x.shape
    q, k32, kb, v32, vb = _proj(x, sc, sh, bw["w"], [[BF16], [F32, BF16], [F32, BF16]], name="proj_sb")
    tq, tk = _attn_tiles(s)
    if cache is None:
        off, kall, vall = 0, kb, vb
    else:
        ck, cv = cache
        off = ck.shape[1]
        flat = lambda a: a.reshape(a.shape[0], a.shape[1], -1).astype(BF16)
        kall = jnp.concatenate([flat(ck), kb], axis=1)
        vall = jnp.concatenate([flat(cv), vb], axis=1)
    skp = _round_up(off + s, tk)
    o = _sb_attn(q, _pad_rows(kall, skp), _pad_rows(vall, skp), off, tq, tk)
    hshape = (b, s, N_HEADS, HEAD_DIM)
    return o, (k32.reshape(hshape), v32.reshape(hshape))


_MIXERS = (_fox_mixer, _mla_mixer, _swa_mixer, _sb_mixer)


def _trunk(x, mods, caches, wts):
    states = []
    for i in range(DEPTH):
        m = [mods[i, :, j][:, None, :] for j in range(6)]
        sh_a, sc_a, g_a, sh_f, sc_f, g_f = m
        o, st = _MIXERS[i](x, sc_a, sh_a, wts["mixer"][i], caches[i])
        states.extend(st)
        x1, h2, gates = _outproj(o, x, wts["w_out"][i], g_a, sc_f, sh_f,
                                 wts["ln_g"][i, 0:1], wts["ln_b"][i, 0:1],
                                 wts["rw_hi"], wts["rw_lo"], wts["rb"])
        x = _moe(h2, gates, x1, g_f, wts["ln_g"][i, 1:2], wts["ln_b"][i, 1:2],
                 wts["moe_g"][i], wts["moe_u"][i], wts["moe_d"][i])
    return x, states


def kernel(x_prompt, x_sample, cache_fox_k, cache_fox_v, cache_fox_logf, cache_mla_ckv, cache_mla_krope, cache_swa_k, cache_swa_v, cache_sb_k, cache_sb_v, c_prompt, c_sample, ada_w, ada_b, ln_g, ln_b, fox_w_in, fox_b_f, fox_w_out, mla_w_in, mla_q_norm, mla_w_uq, mla_kv_norm, mla_w_uk, mla_w_uv, mla_w_out, swa_w_in, swa_sinks, swa_w_out, sb_w_in, sb_w_out, router_w, router_b, moe_w_gate, moe_w_up, moe_w_down):
    bp, sp, d = x_prompt.shape
    bs, ss, _ = x_sample.shape
    past = cache_fox_k.shape[1]
    hw = N_HEADS * HEAD_DIM
    qscale = HEAD_DIM ** -0.5

    fox = dict(w=[(fox_w_in[:, :hw] * qscale).astype(BF16), fox_w_in[:, hw:2 * hw].astype(BF16),
                  fox_w_in[:, 2 * hw:3 * hw].astype(BF16), _pad_cols(fox_w_in[:, 3 * hw:], LANES).astype(BF16)],
               bias=_pad_cols(fox_b_f.reshape(1, -1), LANES))
    mla = _mla_weights(mla_w_in, mla_q_norm, mla_w_uq, mla_kv_norm, mla_w_uk, mla_w_uv)
    mla["tables"] = _rope_tables(max(sp, past + ss))
    kvw = SWA_KV_HEADS * HEAD_DIM
    wk, wv = swa_w_in[:, hw:hw + kvw], swa_w_in[:, hw + kvw:]
    swa = dict(w=[(swa_w_in[:, :hw] * qscale).astype(BF16), wk.astype(BF16), _dup_heads(wk).astype(BF16),
                  wv.astype(BF16), _dup_heads(wv).astype(BF16)],
               slopes=jnp.exp2(-8.0 * jnp.arange(1, N_HEADS + 1, dtype=F32) / N_HEADS),
               sinks=swa_sinks.astype(F32))
    sb = dict(w=[(sb_w_in[:, :hw] * qscale).astype(BF16), sb_w_in[:, hw:2 * hw].astype(BF16),
                 sb_w_in[:, 2 * hw:].astype(BF16)])
    rw = _pad_cols(router_w, LANES)
    rw_hi = rw.astype(BF16)
    wts = dict(mixer=(fox, mla, swa, sb),
               w_out=[w.astype(BF16) for w in (fox_w_out, mla_w_out, swa_w_out, sb_w_out)],
               ln_g=ln_g, ln_b=ln_b, rw_hi=rw_hi, rw_lo=(rw - rw_hi.astype(F32)).astype(BF16),
               rb=_pad_cols(router_b.reshape(1, -1), LANES),
               moe_g=moe_w_gate.astype(BF16), moe_u=moe_w_up.astype(BF16), moe_d=moe_w_down.astype(BF16))

    mods = _ada_mod(jnp.concatenate([c_prompt, c_sample], axis=0), ada_w, ada_b)
    mods_p = mods[:, :bp].reshape(DEPTH, bp, 6, d)
    mods_s = mods[:, bp:].reshape(DEPTH, bs, 6, d)

    y_p, st_p = _trunk(x_prompt, mods_p, (None, None, None, None), wts)
    caches = ((cache_fox_k, cache_fox_v, cache_fox_logf), (cache_mla_ckv, cache_mla_krope),
              (cache_swa_k, cache_swa_v), (cache_sb_k, cache_sb_v))
    y_s, st_s = _trunk(x_sample, mods_s, caches, wts)
    fox_k_p, fox_v_p, fox_logf_p, mla_ckv_p, mla_krope_p, swa_k_p, swa_v_p, sb_k_p, sb_v_p = st_p
    fox_k_s, fox_v_s, fox_logf_s, mla_ckv_s, mla_krope_s, swa_k_s, swa_v_s, sb_k_s, sb_v_s = st_s
    return (y_p, y_s, fox_k_p, fox_k_s, fox_v_p, fox_v_s, fox_logf_p, fox_logf_s,
            mla_ckv_p, mla_ckv_s, mla_krope_p, mla_krope_s, swa_k_p, swa_k_s, swa_v_p, swa_v_s,
            sb_k_p, sb_k_s, sb_v_p, sb_v_s)
```

```python
import functools
import math

import jax
import jax.numpy as jnp
from jax import lax
from jax.experimental import pallas as pl
from jax.experimental.pallas import tpu as pltpu

F32 = jnp.float32
BF16 = jnp.bfloat16

D_MODEL = 1024
DEPTH = 4
CHUNK = 64
HEAD_DIM = 64
N_HEADS = 16
N_PAIRS = N_HEADS // 2
LANES = 128
MLA_Q_RANK = 384
MLA_KV_RANK = 256
MLA_D_NOPE = 64
MLA_D_ROPE = 32
ROPE_THETA = 10000.0
SWA_KV_HEADS = 4
SWA_GROUPS = 4
WINDOW = 128
N_EXPERTS = 16
D_EXPERT = 256
DEEPNORM_ALPHA = (2.0 * DEPTH) ** 0.25
NEG = -1e30
LOG2E = 1.4426950408889634
LONG_TQ = 512
VMEM_LIMIT = 56 * 1024 * 1024


def _cparams(*sem):
    return pltpu.CompilerParams(dimension_semantics=sem, vmem_limit_bytes=VMEM_LIMIT)


def _dot(a, b):
    return jnp.dot(a, b, preferred_element_type=F32)


def _dot_nt(a, b):
    return lax.dot_general(a, b, (((1,), (1,)), ((), ())), preferred_element_type=F32)


def _dot_f32(a, b):
    return lax.dot_general(a, b, (((1,), (0,)), ((), ())), precision=lax.Precision.HIGHEST,
                           preferred_element_type=F32)


def _sigmoid(x):
    return 1.0 / (1.0 + jnp.exp(-x))


def _log_sigmoid(x):
    return jnp.minimum(x, 0.0) - jnp.log1p(jnp.exp(-jnp.abs(x)))


def _softplus(x):
    return jnp.maximum(x, 0.0) + jnp.log(1.0 + jnp.exp2(jnp.abs(x) * -LOG2E))


def _layer_norm(y, g, b):
    mu = jnp.mean(y, axis=-1, keepdims=True)
    yc = y - mu
    var = jnp.mean(yc * yc, axis=-1, keepdims=True)
    return yc * lax.rsqrt(var + 1e-5) * g + b


def _ada_kernel(c_ref, w_ref, b_ref, o_ref):
    c = c_ref[...]
    o_ref[0] = _dot_f32(c * _sigmoid(c), w_ref[0]) + b_ref[0]


def _ada_mod(c, ada_w, ada_b):
    n, d = c.shape
    depth, _, n6 = ada_w.shape
    tn = 1536
    return pl.pallas_call(
        _ada_kernel,
        grid=(depth, n6 // tn),
        in_specs=[pl.BlockSpec((n, d), lambda i, j: (0, 0)),
                  pl.BlockSpec((1, d, tn), lambda i, j: (i, 0, j)),
                  pl.BlockSpec((1, 1, tn), lambda i, j: (i, 0, j))],
        out_specs=pl.BlockSpec((1, n, tn), lambda i, j: (i, 0, j)),
        out_shape=jax.ShapeDtypeStruct((depth, n, n6), F32),
        compiler_params=_cparams("parallel", "parallel"),
        name="ada_mod",
    )(c, ada_w, ada_b.reshape(depth, 1, n6))


def _proj_kernel(*refs, n_w, emits, has_bias):
    x_ref, sc_ref, sh_ref = refs[:3]
    w_refs = refs[3:3 + n_w]
    pos = 3 + n_w
    bias_ref = refs[pos] if has_bias else None
    o_refs = refs[pos + (1 if has_bias else 0):]
    h = (x_ref[0] * (1.0 + sc_ref[0]) + sh_ref[0]).astype(BF16)
    oi = 0
    for j in range(n_w):
        y = _dot(h, w_refs[j][...])
        for kind in emits[j]:
            if kind == "logf":
                o_refs[oi][0] = _log_sigmoid(y + bias_ref[...])[:, :N_HEADS]
            else:
                o_refs[oi][0] = y.astype(kind)
            oi += 1


def _row_tile(s, target):
    return min(s, target)


def _proj(x, sc, sh, weights, emits, bias=None, tm=512, name="proj"):
    b, s, d = x.shape
    tm = _row_tile(s, tm)
    in_specs = [pl.BlockSpec((1, tm, d), lambda i, j: (i, j, 0)),
                pl.BlockSpec((1, 1, d), lambda i, j: (i, 0, 0)),
                pl.BlockSpec((1, 1, d), lambda i, j: (i, 0, 0))]
    args = [x, sc, sh]
    for w in weights:
        in_specs.append(pl.BlockSpec(w.shape, lambda i, j: (0, 0)))
        args.append(w)
    if bias is not None:
        in_specs.append(pl.BlockSpec(bias.shape, lambda i, j: (0, 0)))
        args.append(bias)
    out_specs, out_shape = [], []
    for w, em in zip(weights, emits):
        for kind in em:
            n, dt = (N_HEADS, F32) if kind == "logf" else (w.shape[1], kind)
            out_specs.append(pl.BlockSpec((1, tm, n), lambda i, j: (i, j, 0)))
            out_shape.append(jax.ShapeDtypeStruct((b, s, n), dt))
    return pl.pallas_call(
        functools.partial(_proj_kernel, n_w=len(weights), emits=emits, has_bias=bias is not None),
        grid=(b, s // tm),
        in_specs=in_specs, out_specs=out_specs, out_shape=out_shape,
        compiler_params=_cparams("parallel", "parallel"),
        name=name,
    )(*args)


def _cumsum_kernel(x_ref, f_ref, pad_ref):
    nblk = x_ref.shape[1] // LANES
    r = lax.broadcasted_iota(jnp.int32, (LANES, LANES), 0)
    c = lax.broadcasted_iota(jnp.int32, (LANES, LANES), 1)
    tri = (c <= r).astype(F32)
    pad_ref[...] = jnp.zeros_like(pad_ref)

    def body(i, carry):
        r0 = pl.multiple_of(i * LANES, LANES)
        pad_ref[:, :N_HEADS] = x_ref[0, pl.ds(r0, LANES), :]
        f = _dot_f32(tri, pad_ref[...]) + carry
        f_ref[0, pl.ds(r0, LANES), :] = f[:, :N_HEADS]
        return f[LANES - 1:LANES, :]

    lax.fori_loop(0, nblk, body, jnp.zeros((1, LANES), F32))


def _forget_cumsum(logf):
    b, s, h = logf.shape
    return pl.pallas_call(
        _cumsum_kernel,
        grid=(b,),
        in_specs=[pl.BlockSpec((1, s, h), lambda i: (i, 0, 0))],
        out_specs=pl.BlockSpec((1, s, h), lambda i: (i, 0, 0)),
        out_shape=jax.ShapeDtypeStruct((b, s, h), F32),
        scratch_shapes=[pltpu.VMEM((LANES, LANES), F32)],
        compiler_params=_cparams("parallel"),
        name="forget_cumsum",
    )(logf)


ROW_CHUNK = 128


def _half_mask(shape):
    return lax.broadcasted_iota(jnp.int32, shape, 1) < HEAD_DIM


def _loop(lo, hi, body, carry):
    if isinstance(lo, int) and isinstance(hi, int):
        for i in range(lo, hi):
            carry = body(i, carry)
        return carry
    return lax.fori_loop(lo, hi, body, carry)


def _split3(f):
    f1 = f.astype(BF16).astype(F32)
    r1 = f - f1
    f2 = r1.astype(BF16).astype(F32)
    return f1, f2, r1 - f2


def _attn_kernel(*refs, kind, sq, skp, off, tq, tk, has_cache):
    it = iter(refs)
    q_ref, k_ref, v_ref = next(it), next(it), next(it)
    f_ref = next(it) if kind == "fox" else None
    ck_ref, cv_ref = (next(it), next(it)) if has_cache else (None, None)
    o_ref = next(it)
    kbuf, vbuf = (next(it), next(it)) if has_cache else (None, None)
    ka = (next(it), next(it)) if kind == "fox" else None
    va = (next(it), next(it)) if kind != "sb" else None

    if has_cache:
        past = ck_ref.shape[1]

        def fill(i, _):
            r0 = pl.multiple_of(i * ROW_CHUNK, ROW_CHUNK)
            kbuf[pl.ds(r0, ROW_CHUNK), :] = ck_ref[0, pl.ds(r0, ROW_CHUNK), :].astype(BF16)
            vbuf[pl.ds(r0, ROW_CHUNK), :] = cv_ref[0, pl.ds(r0, ROW_CHUNK), :].astype(BF16)
            return 0

        lax.fori_loop(0, past // ROW_CHUNK, fill, 0)
        kbuf[past:past + sq, :] = k_ref[0]
        vbuf[past:past + sq, :] = v_ref[0]
        if skp > past + sq:
            kbuf[past + sq:, :] = jnp.zeros((skp - past - sq, kbuf.shape[1]), BF16)
            vbuf[past + sq:, :] = jnp.zeros((skp - past - sq, LANES), BF16)
        kget = lambda r0, n: kbuf[pl.ds(r0, n), :]
        vget = lambda r0, n: vbuf[pl.ds(r0, n), :]
    else:
        kget = lambda r0, n: k_ref[0, pl.ds(r0, n), :]
        vget = lambda r0, n: v_ref[0, pl.ds(r0, n), :]

    if kind != "sb":
        lane_c = lax.broadcasted_iota(jnp.int32, (ROW_CHUNK, LANES), 1)

        def build(i, _):
            r0 = pl.multiple_of(i * ROW_CHUNK, ROW_CHUNK)
            vc = vget(r0, ROW_CHUNK)
            if kind == "fox":
                kc = kget(r0, ROW_CHUNK)
            for hh in range(2):
                own = (lane_c < HEAD_DIM) if hh == 0 else (lane_c >= HEAD_DIM)
                va[hh][pl.ds(r0, ROW_CHUNK), :] = jnp.where(own, vc, jnp.ones_like(vc))
                if kind == "fox":
                    base = HEAD_DIM if hh == 0 else 0
                    g1, g2, g3 = _split3(f_ref[0, 0, pl.ds(r0, ROW_CHUNK), hh:hh + 1])
                    e = jnp.where(lane_c == base + 3, -g1,
                                  jnp.where(lane_c == base + 4, -g2,
                                            jnp.where(lane_c == base + 5, -g3,
                                                      jnp.where((lane_c >= base) & (lane_c < base + 3),
                                                                1.0, 0.0))))
                    ka[hh][pl.ds(r0, ROW_CHUNK), :] = jnp.where(own, kc, e.astype(BF16))
            return 0

        lax.fori_loop(0, skp // ROW_CHUNK, build, 0)

    nq = sq // tq
    row = lax.broadcasted_iota(jnp.int32, (tq, tk), 0)
    col = lax.broadcasted_iota(jnp.int32, (tq, tk), 1)
    lane_q = lax.broadcasted_iota(jnp.int32, (tq, LANES), 1)
    lo_half = lane_q < HEAD_DIM
    if kind == "sb":
        kr = lax.broadcasted_iota(jnp.int32, (2 * tk, tk), 0)
        kr = jnp.where(kr >= tk, kr - tk, kr)
        kc_ = lax.broadcasted_iota(jnp.int32, (2 * tk, tk), 1)
        later = jnp.where(kr > kc_, 1.0, 0.0).astype(BF16)

    def qblock(qi, _):
        q0 = qi * tq if isinstance(qi, int) else pl.multiple_of(qi * tq, tq)
        qblk = q_ref[0, pl.ds(q0, tq), :]
        qs = []
        for hh in range(2):
            own = lo_half if hh == 0 else ~lo_half
            if kind == "mla":
                qs.append(qblk[:, hh * LANES:(hh + 1) * LANES])
            elif kind == "sb":
                qs.append(jnp.where(own, qblk, jnp.zeros_like(qblk)))
            else:
                base = HEAD_DIM if hh == 0 else 0
                f1, f2, f3 = _split3(f_ref[0, 0, pl.ds(off + q0, tq), hh:hh + 1])
                e = jnp.where(lane_q == base, f1,
                              jnp.where(lane_q == base + 1, f2,
                                        jnp.where(lane_q == base + 2, f3,
                                                  jnp.where((lane_q >= base + 3) & (lane_q < base + 6),
                                                            1.0, 0.0))))
                qs.append(jnp.where(own, qblk, e.astype(BF16)))

        def kload(hh, k0):
            if kind == "fox":
                return ka[hh][pl.ds(k0, tk), :]
            if kind == "mla":
                return kget(k0, tk)[:, hh * LANES:(hh + 1) * LANES]
            return kget(k0, tk)

        if kind == "sb":
            n_full = (off + q0) // tk
            n_total = (off + q0 + tq - 1 + tk - 1) // tk

            def kstep(jj, carry, masked, base):
                j = base - jj
                k0 = j * tk if isinstance(j, int) else pl.multiple_of(j * tk, tk)
                vb = vget(k0, tk)
                new = []
                for hh in range(2):
                    rem, acc = carry[hh]
                    z = _dot_nt(qs[hh], kload(hh, k0))
                    nl = _softplus(z)
                    if masked:
                        vis = (k0 + col) < (off + q0 + row)
                        nl = jnp.where(vis, nl, 0.0)
                    hi = nl.astype(BF16)
                    lo = (nl - hi.astype(F32)).astype(BF16)
                    suf = _dot(jnp.concatenate([hi, lo], axis=1), later)
                    a = jnp.exp(z - (nl + suf + rem))
                    if masked:
                        a = jnp.where(vis, a, 0.0)
                    new.append((rem + suf[:, 0:1] + nl[:, 0:1], acc + _dot(a.astype(BF16), vb)))
                return tuple(new)

            init = ((jnp.zeros((tq, 1), F32), jnp.zeros((tq, LANES), F32)),) * 2
            carry = _loop(0, n_total - n_full,
                          functools.partial(kstep, masked=True, base=n_total - 1), init)
            carry = _loop(0, n_full, functools.partial(kstep, masked=False, base=n_full - 1), carry)
            outs = [c[1] for c in carry]
        else:
            if kind == "fox":
                n_full = (off + q0 + 1) // tk
                n_total = (off + q0 + tq + tk - 1) // tk
            else:
                n_full = ((off + q0) // CHUNK * CHUNK + CHUNK) // tk
                n_total = (((off + q0 + tq - 1) // CHUNK + 1) * CHUNK + tk - 1) // tk

            def kstep(j, carry, masked):
                k0 = j * tk if isinstance(j, int) else pl.multiple_of(j * tk, tk)
                new = []
                for hh in range(2):
                    m, acc = carry[hh]
                    s = _dot_nt(qs[hh], kload(hh, k0))
                    if masked:
                        qpos = off + q0 + row
                        kpos = k0 + col
                        vis = (kpos <= qpos) if kind == "fox" else ((kpos // CHUNK) <= (qpos // CHUNK))
                        s = jnp.where(vis, s, NEG)
                    m_new = jnp.maximum(m, jnp.max(s, axis=-1, keepdims=True))
                    p = jnp.exp(s - m_new)
                    pv = _dot(p.astype(BF16), va[hh][pl.ds(k0, tk), :])
                    new.append((m_new, jnp.exp(m - m_new) * acc + pv))
                return tuple(new)

            init = ((jnp.full((tq, 1), NEG, F32), jnp.zeros((tq, LANES), F32)),) * 2
            carry = _loop(0, n_full, functools.partial(kstep, masked=False), init)
            carry = _loop(n_full, n_total, functools.partial(kstep, masked=True), carry)
            outs = [c[1] / pltpu.roll(c[1], HEAD_DIM, 1) for c in carry]
        o_ref[0, pl.ds(q0, tq), :] = jnp.where(lo_half, outs[0], outs[1]).astype(o_ref.dtype)
        return 0

    _loop(0, nq, qblock, 0) if nq == 1 else lax.fori_loop(0, nq, qblock, 0)


def _attention(kind, q, k, v, off, tq, tk, skp, f=None, cache=None):
    b, sq, qw = q.shape
    sn = k.shape[1]
    ql = qw // N_PAIRS
    pair = lambda n, w: pl.BlockSpec((1, n, w), lambda i, p: (i, 0, p))
    in_specs = [pair(sq, ql), pair(sn, ql), pair(sn, LANES)]
    args = [q, k, v]
    scratch = []
    if kind == "fox":
        in_specs.append(pl.BlockSpec((1, 1, skp, 2), lambda i, p: (i, p, 0, 0)))
        args.append(f)
    if cache is not None:
        ck, cv = cache
        in_specs += [pair(ck.shape[1], ql), pair(cv.shape[1], LANES)]
        args += [ck, cv]
        scratch += [pltpu.VMEM((skp, ql), BF16), pltpu.VMEM((skp, LANES), BF16)]
    else:
        assert skp == sn
    if kind == "fox":
        scratch += [pltpu.VMEM((skp, LANES), BF16)] * 2
    if kind != "sb":
        scratch += [pltpu.VMEM((skp, LANES), BF16)] * 2
    return pl.pallas_call(
        functools.partial(_attn_kernel, kind=kind, sq=sq, skp=skp, off=off, tq=tq, tk=tk,
                          has_cache=cache is not None),
        grid=(b, N_PAIRS),
        in_specs=in_specs,
        out_specs=pair(sq, LANES),
        out_shape=jax.ShapeDtypeStruct((b, sq, N_HEADS * HEAD_DIM), BF16),
        scratch_shapes=scratch,
        compiler_params=_cparams("parallel", "parallel"),
        name="attn_" + kind,
    )(*args)


def _attn_t_kernel(*refs, kind, sq, off, tq, tk):
    it = iter(refs)
    q_ref, k_ref, v_ref = next(it), next(it), next(it)
    f_ref = next(it) if kind == "fox" else None
    o_ref = next(it)
    ka = (next(it), next(it)) if kind == "fox" else None
    vt = (next(it), next(it)) if kind != "sb" else (next(it),)
    skp = k_ref.shape[1]

    lane_c = lax.broadcasted_iota(jnp.int32, (ROW_CHUNK, LANES), 1)
    chan = lax.broadcasted_iota(jnp.int32, (LANES, ROW_CHUNK), 0)

    def build(i, _):
        r0 = pl.multiple_of(i * ROW_CHUNK, ROW_CHUNK)
        vct = v_ref[0, pl.ds(r0, ROW_CHUNK), :].astype(F32).T
        if kind == "sb":
            vt[0][:, pl.ds(r0, ROW_CHUNK)] = vct.astype(BF16)
        else:
            vt[0][:, pl.ds(r0, ROW_CHUNK)] = jnp.where(chan < HEAD_DIM, vct, 1.0).astype(BF16)
            vt[1][:, pl.ds(r0, ROW_CHUNK)] = jnp.where(chan >= HEAD_DIM, vct, 1.0).astype(BF16)
        if kind == "fox":
            kc = k_ref[0, pl.ds(r0, ROW_CHUNK), :]
            for hh in range(2):
                own = (lane_c < HEAD_DIM) if hh == 0 else (lane_c >= HEAD_DIM)
                base = HEAD_DIM if hh == 0 else 0
                g1, g2, g3 = _split3(f_ref[0, 0, pl.ds(r0, ROW_CHUNK), hh:hh + 1])
                e = jnp.where(lane_c == base + 3, -g1,
                              jnp.where(lane_c == base + 4, -g2,
                                        jnp.where(lane_c == base + 5, -g3,
                                                  jnp.where((lane_c >= base) & (lane_c < base + 3),
                                                            1.0, 0.0))))
                ka[hh][pl.ds(r0, ROW_CHUNK), :] = jnp.where(own, kc, e.astype(BF16))
        return 0

    lax.fori_loop(0, skp // ROW_CHUNK, build, 0)

    nq = sq // tq
    krow = lax.broadcasted_iota(jnp.int32, (tk, tq), 0)
    qcol = lax.broadcasted_iota(jnp.int32, (tk, tq), 1)
    lane_q = lax.broadcasted_iota(jnp.int32, (tq, LANES), 1)
    lo_half = lane_q < HEAD_DIM
    out_lo = lax.broadcasted_iota(jnp.int32, (LANES, tq), 0) < HEAD_DIM
    if kind == "sb":
        a_ = lax.broadcasted_iota(jnp.int32, (LANES, 2 * LANES), 0)
        b_ = lax.broadcasted_iota(jnp.int32, (LANES, 2 * LANES), 1)
        b_ = jnp.where(b_ >= LANES, b_ - LANES, b_)
        later_t = jnp.where(b_ > a_, 1.0, 0.0).astype(BF16)

    def qblock(qi, _):
        q0 = pl.multiple_of(qi * tq, tq)
        qblk = q_ref[0, pl.ds(q0, tq), :]
        qs = []
        for hh in range(2):
            own = lo_half if hh == 0 else ~lo_half
            if kind == "mla":
                qs.append(qblk[:, hh * LANES:(hh + 1) * LANES])
            elif kind == "sb":
                qs.append(jnp.where(own, qblk, jnp.zeros_like(qblk)))
            else:
                base = HEAD_DIM if hh == 0 else 0
                f1, f2, f3 = _split3(f_ref[0, 0, pl.ds(off + q0, tq), hh:hh + 1])
                e = jnp.where(lane_q == base, f1,
                              jnp.where(lane_q == base + 1, f2,
                                        jnp.where(lane_q == base + 2, f3,
                                                  jnp.where((lane_q >= base + 3) & (lane_q < base + 6),
                                                            1.0, 0.0))))
                qs.append(jnp.where(own, qblk, e.astype(BF16)))

        def kload(hh, k0):
            if kind == "fox":
                return ka[hh][pl.ds(k0, tk), :]
            if kind == "mla":
                return k_ref[0, pl.ds(k0, tk), hh * LANES:(hh + 1) * LANES]
            return k_ref[0, pl.ds(k0, tk), :]

        if kind == "sb":
            n_full = (off + q0) // tk
            n_total = (off + q0 + tq - 1 + tk - 1) // tk

            def kstep(jj, carry, masked, base):
                j = base - jj
                k0 = pl.multiple_of(j * tk, tk)
                vb = vt[0][:, pl.ds(k0, tk)]
                new = []
                for hh in range(2):
                    rem, acc = carry[hh]
                    z = _dot_nt(kload(hh, k0), qs[hh])
                    nl = _softplus(z)
                    if masked:
                        vis = (k0 + krow) < (off + q0 + qcol)
                        nl = jnp.where(vis, nl, 0.0)
                    hi = nl.astype(BF16)
                    lo = (nl - hi.astype(F32)).astype(BF16)
                    after = rem
                    parts = []
                    for sb in reversed(range(tk // LANES)):
                        rs = slice(sb * LANES, (sb + 1) * LANES)
                        within = _dot(later_t, jnp.concatenate([hi[rs], lo[rs]], axis=0))
                        parts.append(within + after)
                        after = after + within[0:1, :] + nl[sb * LANES:sb * LANES + 1, :]
                    suf = jnp.concatenate(parts[::-1], axis=0)
                    a = jnp.exp(z - (nl + suf))
                    if masked:
                        a = jnp.where(vis, a, 0.0)
                    new.append((after, acc + _dot(vb, a.astype(BF16))))
                return tuple(new)

            init = ((jnp.zeros((1, tq), F32), jnp.zeros((LANES, tq), F32)),) * 2
            carry = lax.fori_loop(0, n_total - n_full,
                                  functools.partial(kstep, masked=True, base=n_total - 1), init)
            carry = lax.fori_loop(0, n_full,
                                  functools.partial(kstep, masked=False, base=n_full - 1), carry)
            outs = [c[1] for c in carry]
        else:
            if kind == "fox":
                n_full = (off + q0 + 1) // tk
                n_total = (off + q0 + tq + tk - 1) // tk
            else:
                n_full = ((off + q0) // CHUNK * CHUNK + CHUNK) // tk
                n_total = (((off + q0 + tq - 1) // CHUNK + 1) * CHUNK + tk - 1) // tk

            def kstep(j, carry, masked):
                k0 = pl.multiple_of(j * tk, tk)
                new = []
                for hh in range(2):
                    m, acc = carry[hh]
                    s = _dot_nt(kload(hh, k0), qs[hh])
                    if masked:
                        qpos = off + q0 + qcol
                        kpos = k0 + krow
                        vis = (kpos <= qpos) if kind == "fox" else ((kpos // CHUNK) <= (qpos // CHUNK))
                        s = jnp.where(vis, s, NEG)
                    m_new = jnp.maximum(m, jnp.max(s, axis=0, keepdims=True))
                    p = jnp.exp(s - m_new)
                    pv = _dot(vt[hh][:, pl.ds(k0, tk)], p.astype(BF16))
                    new.append((m_new, jnp.exp(m - m_new) * acc + pv))
                return tuple(new)

            init = ((jnp.full((1, tq), NEG, F32), jnp.zeros((LANES, tq), F32)),) * 2
            carry = lax.fori_loop(0, n_full, functools.partial(kstep, masked=False), init)
            carry = lax.fori_loop(n_full, n_total, functools.partial(kstep, masked=True), carry)
            outs = [carry[0][1] / carry[0][1][HEAD_DIM:HEAD_DIM + 1, :],
                    carry[1][1] / carry[1][1][0:1, :]]
        o_ref[0, pl.ds(q0, tq), :] = jnp.where(out_lo, outs[0], outs[1]).T.astype(o_ref.dtype)
        return 0

    lax.fori_loop(0, nq, qblock, 0)


def _attention_t(kind, q, k, v, off, tq, tk, f=None):
    b, sq, qw = q.shape
    skp = k.shape[1]
    ql = qw // N_PAIRS
    pair = lambda n, w: pl.BlockSpec((1, n, w), lambda i, p: (i, 0, p))
    in_specs = [pair(sq, ql), pair(skp, ql), pair(skp, LANES)]
    args = [q, k, v]
    scratch = []
    if kind == "fox":
        in_specs.append(pl.BlockSpec((1, 1, skp, 2), lambda i, p: (i, p, 0, 0)))
        args.append(f)
        scratch += [pltpu.VMEM((skp, LANES), BF16)] * 2
    scratch += [pltpu.VMEM((LANES, skp), BF16)] * (1 if kind == "sb" else 2)
    return pl.pallas_call(
        functools.partial(_attn_t_kernel, kind=kind, sq=sq, off=off, tq=tq, tk=tk),
        grid=(b, N_PAIRS),
        in_specs=in_specs,
        out_specs=pair(sq, LANES),
        out_shape=jax.ShapeDtypeStruct((b, sq, N_HEADS * HEAD_DIM), BF16),
        scratch_shapes=scratch,
        compiler_params=_cparams("parallel", "parallel"),
        name="attn_t_" + kind,
    )(*args)


SWA_SPAN = 2 * WINDOW


def _swa_kernel(slope_ref, sink_ref, q_ref, k_ref, v_ref, o_ref, *, sq, past, tq):
    nq = sq // tq
    kv = pl.program_id(1)
    row = lax.broadcasted_iota(jnp.int32, (tq, SWA_SPAN), 0)
    col = lax.broadcasted_iota(jnp.int32, (tq, SWA_SPAN), 1)
    lo_half = _half_mask((tq, LANES))

    def qblock(qi, _):
        q0 = pl.multiple_of(qi * tq, tq)
        ks = pl.multiple_of(jnp.maximum(q0 + past - WINDOW, 0), CHUNK)
        kwin = k_ref[0, pl.ds(ks, SWA_SPAN), :]
        vwin = v_ref[0, pl.ds(ks, SWA_SPAN), :]
        qpos = q0 + row
        kpos = ks - past + col
        qc = qpos // CHUNK
        vis = (kpos >= qc * CHUNK - WINDOW) & (kpos < (qc + 1) * CHUNK)
        dist = jnp.abs(qpos - kpos).astype(F32)
        for pair in range(SWA_GROUPS // 2):
            qpair = q_ref[0, pl.ds(q0, tq), pair * LANES:(pair + 1) * LANES]
            outs = []
            for half in range(2):
                head = kv * SWA_GROUPS + pair * 2 + half
                qh = jnp.where(lo_half if half == 0 else ~lo_half, qpair, jnp.zeros_like(qpair))
                s = _dot_nt(qh, kwin) - slope_ref[head] * dist
                s = jnp.where(vis, s, NEG)
                sink = sink_ref[head]
                m = jnp.maximum(jnp.max(s, axis=-1, keepdims=True), sink)
                e = jnp.exp(s - m)
                den = jnp.sum(e, axis=-1, keepdims=True) + jnp.exp(sink - m)
                outs.append(_dot((e / den).astype(BF16), vwin))
            o_ref[0, pl.ds(q0, tq), pair * LANES:(pair + 1) * LANES] = (
                jnp.where(lo_half, outs[0], outs[1]).astype(o_ref.dtype))
        return 0

    lax.fori_loop(0, nq, qblock, 0)


def _swa_attn(q, kdup, vdup, slopes, sinks, past, tq):
    b, sq, _ = q.shape
    sk = kdup.shape[1]
    gw = SWA_GROUPS * HEAD_DIM
    smem = pl.BlockSpec(memory_space=pltpu.SMEM)
    return pl.pallas_call(
        functools.partial(_swa_kernel, sq=sq, past=past, tq=tq),
        grid=(b, SWA_KV_HEADS),
        in_specs=[smem, smem,
                  pl.BlockSpec((1, sq, gw), lambda i, p: (i, 0, p)),
                  pl.BlockSpec((1, sk, LANES), lambda i, p: (i, 0, p)),
                  pl.BlockSpec((1, sk, LANES), lambda i, p: (i, 0, p))],
        out_specs=pl.BlockSpec((1, sq, gw), lambda i, p: (i, 0, p)),
        out_shape=jax.ShapeDtypeStruct((b, sq, N_HEADS * HEAD_DIM), BF16),
        compiler_params=_cparams("parallel", "parallel"),
        name="attn_swa",
    )(slopes, sinks, q, kdup, vdup)


MLA_IN_COLS = MLA_Q_RANK + MLA_KV_RANK + 2 * LANES
MLA_HEAD_W = LANES


def _rms_norm(x, g):
    return x * lax.rsqrt(jnp.mean(x * x, axis=-1, keepdims=True) + 1e-6) * g


def _mla_proj_kernel(x_ref, sc_ref, sh_ref, win_ref, gq_ref, gkv_ref, wqa_ref, wqb_ref,
                     wk_ref, wv_ref, qc_ref, qs_ref, kc_ref, ks_ref,
                     q_ref, kf_ref, v_ref, ckv_ref, kr_ref):
    h = (x_ref[0] * (1.0 + sc_ref[0]) + sh_ref[0]).astype(BF16)
    proj = _dot(h, win_ref[...])
    cq = _rms_norm(proj[:, :MLA_Q_RANK], gq_ref[...]).astype(BF16)
    ckv = _rms_norm(proj[:, MLA_Q_RANK:MLA_Q_RANK + MLA_KV_RANK], gkv_ref[...])
    ckv_ref[0] = ckv
    ckv_b = ckv.astype(BF16)
    kr0 = MLA_Q_RANK + MLA_KV_RANK
    krr = proj[:, kr0:kr0 + LANES] * kc_ref[...] + proj[:, kr0 + LANES:kr0 + 2 * LANES] * ks_ref[...]
    kr_ref[0] = pltpu.roll(krr, LANES - MLA_D_NOPE, 1)[:, :MLA_D_ROPE]
    qa = _dot(cq, wqa_ref[...])
    qb = _dot(cq, wqb_ref[...])
    kn = _dot(ckv_b, wk_ref[...])
    v_ref[0] = _dot(ckv_b, wv_ref[...]).astype(BF16)
    qc, qs = qc_ref[...], qs_ref[...]
    for hd in range(N_HEADS):
        sl = slice(hd * MLA_HEAD_W, (hd + 1) * MLA_HEAD_W)
        q_ref[0, :, sl] = (qa[:, sl] * qc + qb[:, sl] * qs).astype(BF16)
        kf_ref[0, :, sl] = (kn[:, sl] + krr).astype(BF16)


def _mla_proj(x, sc, sh, mw, tables, tm=256):
    b, s, d = x.shape
    tm = _row_tile(s, tm)
    qc, qs, kc, ks = tables
    full = lambda a: pl.BlockSpec(a.shape, lambda i, j: (0,) * a.ndim)
    tab = pl.BlockSpec((tm, LANES), lambda i, j: (j, 0))
    row = lambda n: pl.BlockSpec((1, tm, n), lambda i, j: (i, j, 0))
    hw = N_HEADS * MLA_HEAD_W
    return pl.pallas_call(
        _mla_proj_kernel,
        grid=(b, s // tm),
        in_specs=[row(d),
                  pl.BlockSpec((1, 1, d), lambda i, j: (i, 0, 0)),
                  pl.BlockSpec((1, 1, d), lambda i, j: (i, 0, 0)),
                  full(mw["win"]), full(mw["gq"]), full(mw["gkv"]), full(mw["wqa"]),
                  full(mw["wqb"]), full(mw["wk"]), full(mw["wv"]), tab, tab, tab, tab],
        out_specs=[row(hw), row(hw), row(N_HEADS * HEAD_DIM), row(MLA_KV_RANK), row(MLA_D_ROPE)],
        out_shape=[jax.ShapeDtypeStruct((b, s, hw), BF16),
                   jax.ShapeDtypeStruct((b, s, hw), BF16),
                   jax.ShapeDtypeStruct((b, s, N_HEADS * HEAD_DIM), BF16),
                   jax.ShapeDtypeStruct((b, s, MLA_KV_RANK), F32),
                   jax.ShapeDtypeStruct((b, s, MLA_D_ROPE), F32)],
        compiler_params=_cparams("parallel", "parallel"),
        name="proj_mla",
    )(x, sc, sh, mw["win"], mw["gq"], mw["gkv"], mw["wqa"], mw["wqb"], mw["wk"], mw["wv"],
      qc, qs, kc, ks)


def _mla_cache_kernel(ckv_ref, kr_ref, wk_ref, wv_ref, place_ref, kf_ref, v_ref):
    ckv_b = ckv_ref[0].astype(BF16)
    kf_ref[0] = (_dot(ckv_b, wk_ref[...]) + _dot(kr_ref[0], place_ref[...])).astype(BF16)
    v_ref[0] = _dot(ckv_b, wv_ref[...]).astype(BF16)


def _mla_cache_up(ckv, kr_pad, mw, tm=512):
    b, p, r = ckv.shape
    tm = _row_tile(p, tm)
    hw = N_HEADS * MLA_HEAD_W
    full = lambda a: pl.BlockSpec(a.shape, lambda i, j: (0,) * a.ndim)
    row = lambda n: pl.BlockSpec((1, tm, n), lambda i, j: (i, j, 0))
    return pl.pallas_call(
        _mla_cache_kernel,
        grid=(b, p // tm),
        in_specs=[row(r), row(LANES), full(mw["wk"]), full(mw["wv"]), full(mw["place"])],
        out_specs=[row(hw), row(N_HEADS * HEAD_DIM)],
        out_shape=[jax.ShapeDtypeStruct((b, p, hw), BF16),
                   jax.ShapeDtypeStruct((b, p, N_HEADS * HEAD_DIM), BF16)],
        compiler_params=_cparams("parallel", "parallel"),
        name="mla_cache_up",
    )(ckv, kr_pad, mw["wk"], mw["wv"], mw["place"])


def _xor_partner(x, k, lane):
    up = pltpu.roll(x, LANES - k, 1)
    dn = pltpu.roll(x, k, 1)
    return jnp.where((lane & k) == 0, up, dn)


def _lane_argmax(v, idx, lane, strides):
    for k in strides:
        pv = _xor_partner(v, k, lane)
        pi = _xor_partner(idx, k, lane)
        take = (pv > v) | ((pv == v) & (pi < idx))
        v = jnp.where(take, pv, v)
        idx = jnp.where(take, pi, idx)
    return v, idx


def _route(logits, rb):
    lane = lax.broadcasted_iota(jnp.int32, logits.shape, 1)
    scores = _sigmoid(logits)
    biased = scores + rb
    p1 = _xor_partner(biased, 1, lane)
    hi1, lo1 = jnp.maximum(biased, p1), jnp.minimum(biased, p1)
    hi2, lo2 = _xor_partner(hi1, 2, lane), _xor_partner(lo1, 2, lane)
    group_score = jnp.maximum(hi1, hi2) + jnp.maximum(jnp.minimum(hi1, hi2), jnp.maximum(lo1, lo2))
    gid = lane >> 2
    _, best = _lane_argmax(group_score, gid, lane, (4, 8))
    cand = jnp.where(gid == best, biased, NEG)
    _, i1 = _lane_argmax(cand, lane, lane, (1, 2, 4, 8))
    _, i2 = _lane_argmax(jnp.where(lane == i1, -jnp.inf, cand), lane, lane, (1, 2, 4, 8))
    sel = jnp.where((lane < N_EXPERTS) & ((lane == i1) | (lane == i2)), scores, 0.0)
    return sel / jnp.sum(sel, axis=-1, keepdims=True)


def _outproj_kernel(o_ref, x_ref, w_ref, ga_ref, scf_ref, shf_ref, g_ref, b_ref,
                    rwh_ref, rwl_ref, rb_ref, x1_ref, h2_ref, gate_ref):
    y = DEEPNORM_ALPHA * x_ref[0] + (1.0 + ga_ref[0]) * _dot(o_ref[0], w_ref[...])
    x1 = _layer_norm(y, g_ref[...], b_ref[...])
    x1_ref[0] = x1
    h2 = x1 * (1.0 + scf_ref[0]) + shf_ref[0]
    hi = h2.astype(BF16)
    h2_ref[0] = hi
    lo = (h2 - hi.astype(F32)).astype(BF16)
    logits = _dot(hi, rwh_ref[...]) + _dot(lo, rwh_ref[...]) + _dot(hi, rwl_ref[...])
    gate_ref[0] = _route(logits, rb_ref[...])[:, :N_EXPERTS]


def _outproj(o, x, w, ga, scf, shf, g, bta, rwh, rwl, rb, tm=512):
    b, s, d = x.shape
    tm = _row_tile(s, tm)
    row = lambda n: pl.BlockSpec((1, tm, n), lambda i, j: (i, j, 0))
    mod = pl.BlockSpec((1, 1, d), lambda i, j: (i, 0, 0))
    full = lambda a: pl.BlockSpec(a.shape, lambda i, j: (0,) * a.ndim)
    return pl.pallas_call(
        _outproj_kernel,
        grid=(b, s // tm),
        in_specs=[row(d), row(d), full(w), mod, mod, mod, full(g), full(bta),
                  full(rwh), full(rwl), full(rb)],
        out_specs=[row(d), row(d), row(N_EXPERTS)],
        out_shape=[jax.ShapeDtypeStruct((b, s, d), F32), jax.ShapeDtypeStruct((b, s, d), BF16),
                   jax.ShapeDtypeStruct((b, s, N_EXPERTS), F32)],
        compiler_params=_cparams("parallel", "parallel"),
        name="outproj_ln_router",
    )(o, x, w, ga, scf, shf, g, bta, rwh, rwl, rb)


def _moe_kernel(h_ref, gate_ref, x_ref, gf_ref, g_ref, b_ref, wg_ref, wu_ref, wd_ref,
                o_ref, acc_ref):
    e = pl.program_id(2)

    @pl.when(e == 0)
    def _():
        acc_ref[...] = jnp.zeros_like(acc_ref)

    bt, tm, d = h_ref.shape
    rows = bt * tm
    h = h_ref[...].reshape(rows, d)
    hg = _dot(h, wg_ref[0])
    hu = _dot(h, wu_ref[0])
    gates = gate_ref[...].reshape(rows, N_EXPERTS)
    lane = lax.broadcasted_iota(jnp.int32, gates.shape, 1)
    ge = jnp.sum(jnp.where(lane == e, gates, 0.0), axis=-1, keepdims=True)
    act = hg * _sigmoid(hg) * hu * ge
    acc_ref[...] += _dot(act.astype(BF16), wd_ref[0])

    @pl.when(e == pl.num_programs(2) - 1)
    def _():
        y = DEEPNORM_ALPHA * x_ref[...] + (1.0 + gf_ref[...]) * acc_ref[...].reshape(bt, tm, d)
        o_ref[...] = _layer_norm(y, g_ref[...], b_ref[...])


def _moe(h2, gates, x1, gf, g, bta, wg, wu, wd, rows=512):
    b, s, d = x1.shape
    tm = _row_tile(s, rows)
    bt = min(b, rows // tm)
    ne, _, de = wg.shape
    row = lambda n: pl.BlockSpec((bt, tm, n), lambda i, j, e: (i, j, 0))
    full = lambda a: pl.BlockSpec(a.shape, lambda i, j, e: (0,) * a.ndim)
    return pl.pallas_call(
        _moe_kernel,
        grid=(b // bt, s // tm, ne),
        in_specs=[row(d), row(N_EXPERTS), row(d),
                  pl.BlockSpec((bt, 1, d), lambda i, j, e: (i, 0, 0)), full(g), full(bta),
                  pl.BlockSpec((1, d, de), lambda i, j, e: (e, 0, 0)),
                  pl.BlockSpec((1, d, de), lambda i, j, e: (e, 0, 0)),
                  pl.BlockSpec((1, de, d), lambda i, j, e: (e, 0, 0))],
        out_specs=row(d),
        out_shape=jax.ShapeDtypeStruct((b, s, d), F32),
        scratch_shapes=[pltpu.VMEM((bt * tm, d), F32)],
        compiler_params=_cparams("parallel", "parallel", "arbitrary"),
        name="moe_ffn",
    )(h2, gates, x1, gf, g, bta, wg, wu, wd)


def _pad_cols(w, n):
    return jnp.pad(w, ((0, 0), (0, n - w.shape[1])))


def _rot_half_cols(w):
    half = w.shape[-1] // 2
    return jnp.concatenate([-w[..., half:], w[..., :half]], axis=-1)


def _mla_weights(mla_w_in, mla_q_norm, mla_w_uq, mla_kv_norm, mla_w_uk, mla_w_uv):
    d = mla_w_in.shape[0]
    kr0 = MLA_Q_RANK + MLA_KV_RANK
    wkr = mla_w_in[:, kr0:]
    z = lambda n: jnp.zeros((d, n), F32)
    win = jnp.concatenate([mla_w_in[:, :kr0], z(MLA_D_NOPE), wkr, z(LANES - MLA_D_NOPE - MLA_D_ROPE),
                           z(MLA_D_NOPE), _rot_half_cols(wkr), z(LANES - MLA_D_NOPE - MLA_D_ROPE)],
                          axis=1)
    wuq = mla_w_uq.reshape(MLA_Q_RANK, N_HEADS, MLA_D_NOPE + MLA_D_ROPE)
    zq = lambda n: jnp.zeros((MLA_Q_RANK, N_HEADS, n), F32)
    tail = LANES - MLA_D_NOPE - MLA_D_ROPE
    wqa = jnp.concatenate([wuq, zq(tail)], axis=-1).reshape(MLA_Q_RANK, N_HEADS * LANES)
    wqb = jnp.concatenate([zq(MLA_D_NOPE), _rot_half_cols(wuq[..., MLA_D_NOPE:]), zq(tail)],
                          axis=-1).reshape(MLA_Q_RANK, N_HEADS * LANES)
    wk = jnp.concatenate([mla_w_uk, jnp.zeros((MLA_KV_RANK, N_HEADS, LANES - MLA_D_NOPE), F32)],
                         axis=-1).reshape(MLA_KV_RANK, N_HEADS * LANES)
    wv = mla_w_uv.reshape(MLA_KV_RANK, N_HEADS * HEAD_DIM)
    src = jnp.arange(LANES)[:, None]
    dst = jnp.arange(N_HEADS * LANES)[None, :]
    place = ((dst % LANES) == src + MLA_D_NOPE) & (src < MLA_D_ROPE)
    return dict(win=win.astype(BF16), gq=mla_q_norm.reshape(1, -1), gkv=mla_kv_norm.reshape(1, -1),
                wqa=wqa.astype(BF16), wqb=wqb.astype(BF16), wk=wk.astype(BF16),
                wv=wv.astype(BF16), place=place.astype(BF16))


def _rope_tables(n_pos):
    half = MLA_D_ROPE // 2
    inv = ROPE_THETA ** (-jnp.arange(half, dtype=F32) * 2.0 / MLA_D_ROPE)
    ang = jnp.arange(n_pos, dtype=F32)[:, None] * inv[None, :]
    cos = jnp.concatenate([jnp.cos(ang), jnp.cos(ang)], axis=-1)
    sin = jnp.concatenate([jnp.sin(ang), jnp.sin(ang)], axis=-1)
    scale = (MLA_D_NOPE + MLA_D_ROPE) ** -0.5
    z = lambda n: jnp.zeros((n_pos, n), F32)
    tail = LANES - MLA_D_NOPE - MLA_D_ROPE
    qc = jnp.concatenate([jnp.full((n_pos, MLA_D_NOPE), scale, F32), cos * scale, z(tail)], axis=1)
    qs = jnp.concatenate([z(MLA_D_NOPE), sin * scale, z(tail)], axis=1)
    kc = jnp.concatenate([z(MLA_D_NOPE), cos, z(tail)], axis=1)
    ks = jnp.concatenate([z(MLA_D_NOPE), sin, z(tail)], axis=1)
    return qc, qs, kc, ks


def _dup_heads(w):
    d = w.shape[0]
    w4 = w.reshape(d, SWA_KV_HEADS, HEAD_DIM)
    return jnp.concatenate([w4, w4], axis=-1).reshape(d, SWA_KV_HEADS * LANES)


def _pad_rows(a, n):
    return jnp.pad(a, ((0, 0), (0, n - a.shape[1]), (0, 0)))


def _round_up(n, m):
    return -(-n // m) * m


def _attn_tiles(kind, sq, n_keys):
    if sq % LONG_TQ == 0:
        return LONG_TQ, LONG_TQ, n_keys
    if kind == "sb":
        return sq, 256, _round_up(n_keys, 256)
    skp = _round_up(n_keys, LANES)
    return sq, skp, skp


def _flat_heads(a):
    return a.reshape(a.shape[0], a.shape[1], -1)


def _fox_mixer(x, sc, sh, fw, cache):
    b, s, _ = x.shape
    q, k32, kb, v32, vb, logf = _proj(x, sc, sh, fw["w"], [[BF16], [F32, BF16], [F32, BF16], ["logf"]],
                                      bias=fw["bias"], name="proj_fox")
    off = 0 if cache is None else cache[0].shape[1]
    tq, tk, skp = _attn_tiles("fox", s, off + s)
    lall = logf if cache is None else jnp.concatenate([cache[2], logf], axis=1)
    f = _forget_cumsum(_pad_rows(lall, skp))
    f = f.reshape(b, skp, N_PAIRS, 2).transpose(0, 2, 1, 3)
    kv_cache = None if cache is None else (_flat_heads(cache[0]), _flat_heads(cache[1]))
    if cache is None:
        o = _attention_t("fox", q, kb, vb, off, tq, tk, f=f)
    else:
        o = _attention("fox", q, kb, vb, off, tq, tk, skp, f=f, cache=kv_cache)
    hshape = (b, s, N_HEADS, HEAD_DIM)
    return o, (k32.reshape(hshape), v32.reshape(hshape), logf)


def _mla_mixer(x, sc, sh, mw, cache):
    b, s, _ = x.shape
    off = 0 if cache is None else cache[0].shape[1]
    tables = [t[off:off + s] for t in mw["tables"]]
    q, kf, v, ckv, kr = _mla_proj(x, sc, sh, mw, tables)
    tq, tk, skp = _attn_tiles("mla", s, off + s)
    kv_cache = None
    if cache is not None:
        c_ckv, c_kr = cache
        kr_pad = jnp.pad(c_kr, ((0, 0), (0, 0), (0, LANES - MLA_D_ROPE))).astype(BF16)
        kv_cache = _mla_cache_up(c_ckv, kr_pad, mw)
    if cache is None:
        o = _attention_t("mla", q, kf, v, off, tq, tk)
    else:
        o = _attention("mla", q, kf, v, off, tq, tk, skp, cache=kv_cache)
    return o, (ckv, kr)


def _swa_mixer(x, sc, sh, sw, cache):
    b, s, _ = x.shape
    q, k32, kdup, v32, vdup = _proj(x, sc, sh, sw["w"], [[BF16], [F32], [BF16], [F32], [BF16]],
                                    name="proj_swa")
    kvshape = lambda a: a.reshape(a.shape[0], a.shape[1], SWA_KV_HEADS, HEAD_DIM)
    if cache is None:
        past, tq = 0, 2 * CHUNK
        new_k, new_v = kvshape(k32[:, -WINDOW:]), kvshape(v32[:, -WINDOW:])
    else:
        ck, cv = cache
        past, tq = ck.shape[1], s
        dup = lambda c: jnp.concatenate([c, c], axis=-1).reshape(c.shape[0], c.shape[1], -1).astype(BF16)
        kdup = _pad_rows(jnp.concatenate([dup(ck), kdup], axis=1), SWA_SPAN)
        vdup = _pad_rows(jnp.concatenate([dup(cv), vdup], axis=1), SWA_SPAN)
        new_k = jnp.concatenate([ck, kvshape(k32)], axis=1)[:, -WINDOW:]
        new_v = jnp.concatenate([cv, kvshape(v32)], axis=1)[:, -WINDOW:]
    o = _swa_attn(q, kdup, vdup, sw["slopes"], sw["sinks"], past, tq)
    return o, (new_k, new_v)


def _sb_mixer(x, sc, sh, bw, cache):
    b, s, _ = x.shape
    q, k32, kb, v32, vb = _proj(x, sc, sh, bw["w"], [[BF16], [F32, BF16], [F32, BF16]], name="proj_sb")
    off = 0 if cache is None else cache[0].shape[1]
    tq, tk, skp = _attn_tiles("sb", s, off + s)
    kv_cache = None if cache is None else (_flat_heads(cache[0]), _flat_heads(cache[1]))
    if cache is None:
        o = _attention_t("sb", q, kb, vb, off, tq, tk)
    else:
        o = _attention("sb", q, kb, vb, off, tq, tk, skp, cache=kv_cache)
    hshape = (b, s, N_HEADS, HEAD_DIM)
    return o, (k32.reshape(hshape), v32.reshape(hshape))


_MIXERS = (_fox_mixer, _mla_mixer, _swa_mixer, _sb_mixer)


def _trunk(x, mods, caches, wts):
    states = []
    for i in range(DEPTH):
        m = [mods[i, :, j][:, None, :] for j in range(6)]
        sh_a, sc_a, g_a, sh_f, sc_f, g_f = m
        o, st = _MIXERS[i](x, sc_a, sh_a, wts["mixer"][i], caches[i])
        states.extend(st)
        x1, h2, gates = _outproj(o, x, wts["w_out"][i], g_a, sc_f, sh_f,
                                 wts["ln_g"][i, 0:1], wts["ln_b"][i, 0:1],
                                 wts["rw_hi"], wts["rw_lo"], wts["rb"])
        x = _moe(h2, gates, x1, g_f, wts["ln_g"][i, 1:2], wts["ln_b"][i, 1:2],
                 wts["moe_g"][i], wts["moe_u"][i], wts["moe_d"][i])
    return x, states


def kernel(x_prompt, x_sample, cache_fox_k, cache_fox_v, cache_fox_logf, cache_mla_ckv, cache_mla_krope, cache_swa_k, cache_swa_v, cache_sb_k, cache_sb_v, c_prompt, c_sample, ada_w, ada_b, ln_g, ln_b, fox_w_in, fox_b_f, fox_w_out, mla_w_in, mla_q_norm, mla_w_uq, mla_kv_norm, mla_w_uk, mla_w_uv, mla_w_out, swa_w_in, swa_sinks, swa_w_out, sb_w_in, sb_w_out, router_w, router_b, moe_w_gate, moe_w_up, moe_w_down):
    bp, sp, d = x_prompt.shape
    bs, ss, _ = x_sample.shape
    past = cache_fox_k.shape[1]
    hw = N_HEADS * HEAD_DIM
    qscale = HEAD_DIM ** -0.5

    fox = dict(w=[(fox_w_in[:, :hw] * qscale).astype(BF16), fox_w_in[:, hw:2 * hw].astype(BF16),
                  fox_w_in[:, 2 * hw:3 * hw].astype(BF16), _pad_cols(fox_w_in[:, 3 * hw:], LANES).astype(BF16)],
               bias=_pad_cols(fox_b_f.reshape(1, -1), LANES))
    mla = _mla_weights(mla_w_in, mla_q_norm, mla_w_uq, mla_kv_norm, mla_w_uk, mla_w_uv)
    mla["tables"] = _rope_tables(max(sp, past + ss))
    kvw = SWA_KV_HEADS * HEAD_DIM
    wk, wv = swa_w_in[:, hw:hw + kvw], swa_w_in[:, hw + kvw:]
    swa = dict(w=[(swa_w_in[:, :hw] * qscale).astype(BF16), wk.astype(BF16), _dup_heads(wk).astype(BF16),
                  wv.astype(BF16), _dup_heads(wv).astype(BF16)],
               slopes=jnp.exp2(-8.0 * jnp.arange(1, N_HEADS + 1, dtype=F32) / N_HEADS),
               sinks=swa_sinks.astype(F32))
    sb = dict(w=[(sb_w_in[:, :hw] * qscale).astype(BF16), sb_w_in[:, hw:2 * hw].astype(BF16),
                 sb_w_in[:, 2 * hw:].astype(BF16)])
    rw = _pad_cols(router_w, LANES)
    rw_hi = rw.astype(BF16)
    wts = dict(mixer=(fox, mla, swa, sb),
               w_out=[w.astype(BF16) for w in (fox_w_out, mla_w_out, swa_w_out, sb_w_out)],
               ln_g=ln_g, ln_b=ln_b, rw_hi=rw_hi, rw_lo=(rw - rw_hi.astype(F32)).astype(BF16),
               rb=_pad_cols(router_b.reshape(1, -1), LANES),
               moe_g=moe_w_gate.astype(BF16), moe_u=moe_w_up.astype(BF16), moe_d=moe_w_down.astype(BF16))

    mods = _ada_mod(jnp.concatenate([c_prompt, c_sample], axis=0), ada_w, ada_b)
    mods_p = mods[:, :bp].reshape(DEPTH, bp, 6, d)
    mods_s = mods[:, bp:].reshape(DEPTH, bs, 6, d)

    y_p, st_p = _trunk(x_prompt, mods_p, (None, None, None, None), wts)
    caches = ((cache_fox_k, cache_fox_v, cache_fox_logf), (cache_mla_ckv, cache_mla_krope),
              (cache_swa_k, cache_swa_v), (cache_sb_k, cache_sb_v))
    y_s, st_s = _trunk(x_sample, mods_s, caches, wts)
    fox_k_p, fox_v_p, fox_logf_p, mla_ckv_p, mla_krope_p, swa_k_p, swa_v_p, sb_k_p, sb_v_p = st_p
    fox_k_s, fox_v_s, fox_logf_s, mla_ckv_s, mla_krope_s, swa_k_s, swa_v_s, sb_k_s, sb_v_s = st_s
    return (y_p, y_s, fox_k_p, fox_k_s, fox_v_p, fox_v_s, fox_logf_p, fox_logf_s,
            mla_ckv_p, mla_ckv_s, mla_krope_p, mla_krope_s, swa_k_p, swa_k_s, swa_v_p, swa_v_s,
            sb_k_p, sb_k_s, sb_v_p, sb_v_s)
```

```python
import functools
import math

import jax
import jax.numpy as jnp
from jax import lax
from jax.experimental import pallas as pl
from jax.experimental.pallas import tpu as pltpu

F32 = jnp.float32
BF16 = jnp.bfloat16

D_MODEL = 1024
DEPTH = 4
CHUNK = 64
HEAD_DIM = 64
N_HEADS = 16
N_PAIRS = N_HEADS // 2
LANES = 128
MLA_Q_RANK = 384
MLA_KV_RANK = 256
MLA_D_NOPE = 64
MLA_D_ROPE = 32
ROPE_THETA = 10000.0
SWA_KV_HEADS = 4
SWA_GROUPS = 4
WINDOW = 128
N_EXPERTS = 16
D_EXPERT = 256
DEEPNORM_ALPHA = (2.0 * DEPTH) ** 0.25
NEG = -1e30
LOG2E = 1.4426950408889634
LONG_TQ = 512
VMEM_LIMIT = 56 * 1024 * 1024


def _cparams(*sem):
    return pltpu.CompilerParams(dimension_semantics=sem, vmem_limit_bytes=VMEM_LIMIT)


def _dot(a, b):
    return jnp.dot(a, b, preferred_element_type=F32)


def _dot_nt(a, b):
    return lax.dot_general(a, b, (((1,), (1,)), ((), ())), preferred_element_type=F32)


def _dot_f32(a, b):
    return lax.dot_general(a, b, (((1,), (0,)), ((), ())), precision=lax.Precision.HIGHEST,
                           preferred_element_type=F32)


def _sigmoid(x):
    return 1.0 / (1.0 + jnp.exp(-x))


def _log_sigmoid(x):
    return jnp.minimum(x, 0.0) - jnp.log1p(jnp.exp(-jnp.abs(x)))


def _softplus(x):
    return jnp.maximum(x, 0.0) + jnp.log(1.0 + jnp.exp2(jnp.abs(x) * -LOG2E))


def _layer_norm(y, g, b):
    mu = jnp.mean(y, axis=-1, keepdims=True)
    yc = y - mu
    var = jnp.mean(yc * yc, axis=-1, keepdims=True)
    return yc * lax.rsqrt(var + 1e-5) * g + b


def _ada_kernel(c_ref, w_ref, b_ref, o_ref):
    c = c_ref[...]
    o_ref[0] = _dot_f32(c * _sigmoid(c), w_ref[0]) + b_ref[0]


def _ada_mod(c, ada_w, ada_b):
    n, d = c.shape
    depth, _, n6 = ada_w.shape
    tn = 1536
    return pl.pallas_call(
        _ada_kernel,
        grid=(depth, n6 // tn),
        in_specs=[pl.BlockSpec((n, d), lambda i, j: (0, 0)),
                  pl.BlockSpec((1, d, tn), lambda i, j: (i, 0, j)),
                  pl.BlockSpec((1, 1, tn), lambda i, j: (i, 0, j))],
        out_specs=pl.BlockSpec((1, n, tn), lambda i, j: (i, 0, j)),
        out_shape=jax.ShapeDtypeStruct((depth, n, n6), F32),
        compiler_params=_cparams("parallel", "parallel"),
        name="ada_mod",
    )(c, ada_w, ada_b.reshape(depth, 1, n6))


def _proj_kernel(*refs, n_w, emits, has_bias):
    x_ref, sc_ref, sh_ref = refs[:3]
    w_refs = refs[3:3 + n_w]
    pos = 3 + n_w
    bias_ref = refs[pos] if has_bias else None
    o_refs = refs[pos + (1 if has_bias else 0):]
    h = (x_ref[0] * (1.0 + sc_ref[0]) + sh_ref[0]).astype(BF16)
    oi = 0
    for j in range(n_w):
        y = _dot(h, w_refs[j][...])
        for kind in emits[j]:
            if kind == "logf":
                o_refs[oi][0] = _log_sigmoid(y + bias_ref[...])[:, :N_HEADS]
            else:
                o_refs[oi][0] = y.astype(kind)
            oi += 1


def _row_tile(s, target):
    return min(s, target)


def _proj(x, sc, sh, weights, emits, bias=None, tm=512, name="proj"):
    b, s, d = x.shape
    tm = _row_tile(s, tm)
    in_specs = [pl.BlockSpec((1, tm, d), lambda i, j: (i, j, 0)),
                pl.BlockSpec((1, 1, d), lambda i, j: (i, 0, 0)),
                pl.BlockSpec((1, 1, d), lambda i, j: (i, 0, 0))]
    args = [x, sc, sh]
    for w in weights:
        in_specs.append(pl.BlockSpec(w.shape, lambda i, j: (0, 0)))
        args.append(w)
    if bias is not None:
        in_specs.append(pl.BlockSpec(bias.shape, lambda i, j: (0, 0)))
        args.append(bias)
    out_specs, out_shape = [], []
    for w, em in zip(weights, emits):
        for kind in em:
            n, dt = (N_HEADS, F32) if kind == "logf" else (w.shape[1], kind)
            out_specs.append(pl.BlockSpec((1, tm, n), lambda i, j: (i, j, 0)))
            out_shape.append(jax.ShapeDtypeStruct((b, s, n), dt))
    return pl.pallas_call(
        functools.partial(_proj_kernel, n_w=len(weights), emits=emits, has_bias=bias is not None),
        grid=(b, s // tm),
        in_specs=in_specs, out_specs=out_specs, out_shape=out_shape,
        compiler_params=_cparams("parallel", "parallel"),
        name=name,
    )(*args)


def _cumsum_kernel(x_ref, f_ref, pad_ref):
    nblk = x_ref.shape[1] // LANES
    r = lax.broadcasted_iota(jnp.int32, (LANES, LANES), 0)
    c = lax.broadcasted_iota(jnp.int32, (LANES, LANES), 1)
    tri = (c <= r).astype(F32)
    pad_ref[...] = jnp.zeros_like(pad_ref)

    def body(i, carry):
        r0 = pl.multiple_of(i * LANES, LANES)
        pad_ref[:, :N_HEADS] = x_ref[0, pl.ds(r0, LANES), :]
        f = _dot_f32(tri, pad_ref[...]) + carry
        f_ref[0, pl.ds(r0, LANES), :] = f[:, :N_HEADS]
        return f[LANES - 1:LANES, :]

    lax.fori_loop(0, nblk, body, jnp.zeros((1, LANES), F32))


def _forget_cumsum(logf):
    b, s, h = logf.shape
    return pl.pallas_call(
        _cumsum_kernel,
        grid=(b,),
        in_specs=[pl.BlockSpec((1, s, h), lambda i: (i, 0, 0))],
        out_specs=pl.BlockSpec((1, s, h), lambda i: (i, 0, 0)),
        out_shape=jax.ShapeDtypeStruct((b, s, h), F32),
        scratch_shapes=[pltpu.VMEM((LANES, LANES), F32)],
        compiler_params=_cparams("parallel"),
        name="forget_cumsum",
    )(logf)


ROW_CHUNK = 128


def _half_mask(shape):
    return lax.broadcasted_iota(jnp.int32, shape, 1) < HEAD_DIM


def _loop(lo, hi, body, carry):
    if isinstance(lo, int) and isinstance(hi, int):
        for i in range(lo, hi):
            carry = body(i, carry)
        return carry
    return lax.fori_loop(lo, hi, body, carry)


def _split3(f):
    f1 = f.astype(BF16).astype(F32)
    r1 = f - f1
    f2 = r1.astype(BF16).astype(F32)
    return f1, f2, r1 - f2


def _attn_kernel(*refs, kind, sq, skp, off, tq, tk, has_cache):
    it = iter(refs)
    q_ref, k_ref, v_ref = next(it), next(it), next(it)
    f_ref = next(it) if kind == "fox" else None
    ck_ref, cv_ref = (next(it), next(it)) if has_cache else (None, None)
    o_ref = next(it)
    kbuf, vbuf = (next(it), next(it)) if has_cache else (None, None)
    ka = (next(it), next(it)) if kind == "fox" else None
    va = (next(it), next(it)) if kind != "sb" else None

    if has_cache:
        past = ck_ref.shape[1]

        def fill(i, _):
            r0 = pl.multiple_of(i * ROW_CHUNK, ROW_CHUNK)
            kbuf[pl.ds(r0, ROW_CHUNK), :] = ck_ref[0, pl.ds(r0, ROW_CHUNK), :].astype(BF16)
            vbuf[pl.ds(r0, ROW_CHUNK), :] = cv_ref[0, pl.ds(r0, ROW_CHUNK), :].astype(BF16)
            return 0

        lax.fori_loop(0, past // ROW_CHUNK, fill, 0)
        kbuf[past:past + sq, :] = k_ref[0]
        vbuf[past:past + sq, :] = v_ref[0]
        if skp > past + sq:
            kbuf[past + sq:, :] = jnp.zeros((skp - past - sq, kbuf.shape[1]), BF16)
            vbuf[past + sq:, :] = jnp.zeros((skp - past - sq, LANES), BF16)
        kget = lambda r0, n: kbuf[pl.ds(r0, n), :]
        vget = lambda r0, n: vbuf[pl.ds(r0, n), :]
    else:
        kget = lambda r0, n: k_ref[0, pl.ds(r0, n), :]
        vget = lambda r0, n: v_ref[0, pl.ds(r0, n), :]

    if kind != "sb":
        lane_c = lax.broadcasted_iota(jnp.int32, (ROW_CHUNK, LANES), 1)

        def build(i, _):
            r0 = pl.multiple_of(i * ROW_CHUNK, ROW_CHUNK)
            vc = vget(r0, ROW_CHUNK)
            if kind == "fox":
                kc = kget(r0, ROW_CHUNK)
            for hh in range(2):
                own = (lane_c < HEAD_DIM) if hh == 0 else (lane_c >= HEAD_DIM)
                va[hh][pl.ds(r0, ROW_CHUNK), :] = jnp.where(own, vc, jnp.ones_like(vc))
                if kind == "fox":
                    base = HEAD_DIM if hh == 0 else 0
                    g1, g2, g3 = _split3(f_ref[0, 0, pl.ds(r0, ROW_CHUNK), hh:hh + 1])
                    e = jnp.where(lane_c == base + 3, -g1,
                                  jnp.where(lane_c == base + 4, -g2,
                                            jnp.where(lane_c == base + 5, -g3,
                                                      jnp.where((lane_c >= base) & (lane_c < base + 3),
                                                                1.0, 0.0))))
                    ka[hh][pl.ds(r0, ROW_CHUNK), :] = jnp.where(own, kc, e.astype(BF16))
            return 0

        lax.fori_loop(0, skp // ROW_CHUNK, build, 0)

    nq = sq // tq
    row = lax.broadcasted_iota(jnp.int32, (tq, tk), 0)
    col = lax.broadcasted_iota(jnp.int32, (tq, tk), 1)
    lane_q = lax.broadcasted_iota(jnp.int32, (tq, LANES), 1)
    lo_half = lane_q < HEAD_DIM
    if kind == "sb":
        kr = lax.broadcasted_iota(jnp.int32, (2 * tk, tk), 0)
        kr = jnp.where(kr >= tk, kr - tk, kr)
        kc_ = lax.broadcasted_iota(jnp.int32, (2 * tk, tk), 1)
        later = jnp.where(kr > kc_, 1.0, 0.0).astype(BF16)

    def qblock(qi, _):
        q0 = qi * tq if isinstance(qi, int) else pl.multiple_of(qi * tq, tq)
        qblk = q_ref[0, pl.ds(q0, tq), :]
        qs = []
        for hh in range(2):
            own = lo_half if hh == 0 else ~lo_half
            if kind == "mla":
                qs.append(qblk[:, hh * LANES:(hh + 1) * LANES])
            elif kind == "sb":
                qs.append(jnp.where(own, qblk, jnp.zeros_like(qblk)))
            else:
                base = HEAD_DIM if hh == 0 else 0
                f1, f2, f3 = _split3(f_ref[0, 0, pl.ds(off + q0, tq), hh:hh + 1])
                e = jnp.where(lane_q == base, f1,
                              jnp.where(lane_q == base + 1, f2,
                                        jnp.where(lane_q == base + 2, f3,
                                                  jnp.where((lane_q >= base + 3) & (lane_q < base + 6),
                                                            1.0, 0.0))))
                qs.append(jnp.where(own, qblk, e.astype(BF16)))

        def kload(hh, k0):
            if kind == "fox":
                return ka[hh][pl.ds(k0, tk), :]
            if kind == "mla":
                return kget(k0, tk)[:, hh * LANES:(hh + 1) * LANES]
            return kget(k0, tk)

        if kind == "sb":
            n_full = (off + q0) // tk
            n_total = (off + q0 + tq - 1 + tk - 1) // tk

            def kstep(jj, carry, masked, base):
                j = base - jj
                k0 = j * tk if isinstance(j, int) else pl.multiple_of(j * tk, tk)
                vb = vget(k0, tk)
                new = []
                for hh in range(2):
                    rem, acc = carry[hh]
                    z = _dot_nt(qs[hh], kload(hh, k0))
                    nl = _softplus(z)
                    if masked:
                        vis = (k0 + col) < (off + q0 + row)
                        nl = jnp.where(vis, nl, 0.0)
                    hi = nl.astype(BF16)
                    lo = (nl - hi.astype(F32)).astype(BF16)
                    suf = _dot(jnp.concatenate([hi, lo], axis=1), later)
                    a = jnp.exp(z - (nl + suf + rem))
                    if masked:
                        a = jnp.where(vis, a, 0.0)
                    new.append((rem + suf[:, 0:1] + nl[:, 0:1], acc + _dot(a.astype(BF16), vb)))
                return tuple(new)

            init = ((jnp.zeros((tq, 1), F32), jnp.zeros((tq, LANES), F32)),) * 2
            carry = _loop(0, n_total - n_full,
                          functools.partial(kstep, masked=True, base=n_total - 1), init)
            carry = _loop(0, n_full, functools.partial(kstep, masked=False, base=n_full - 1), carry)
            outs = [c[1] for c in carry]
        else:
            if kind == "fox":
                n_full = (off + q0 + 1) // tk
                n_total = (off + q0 + tq + tk - 1) // tk
            else:
                n_full = ((off + q0) // CHUNK * CHUNK + CHUNK) // tk
                n_total = (((off + q0 + tq - 1) // CHUNK + 1) * CHUNK + tk - 1) // tk

            def kstep(j, carry, masked):
                k0 = j * tk if isinstance(j, int) else pl.multiple_of(j * tk, tk)
                new = []
                for hh in range(2):
                    m, acc = carry[hh]
                    s = _dot_nt(qs[hh], kload(hh, k0))
                    if masked:
                        qpos = off + q0 + row
                        kpos = k0 + col
                        vis = (kpos <= qpos) if kind == "fox" else ((kpos // CHUNK) <= (qpos // CHUNK))
                        s = jnp.where(vis, s, NEG)
                    m_new = jnp.maximum(m, jnp.max(s, axis=-1, keepdims=True))
                    p = jnp.exp(s - m_new)
                    pv = _dot(p.astype(BF16), va[hh][pl.ds(k0, tk), :])
                    new.append((m_new, jnp.exp(m - m_new) * acc + pv))
                return tuple(new)

            init = ((jnp.full((tq, 1), NEG, F32), jnp.zeros((tq, LANES), F32)),) * 2
            carry = _loop(0, n_full, functools.partial(kstep, masked=False), init)
            carry = _loop(n_full, n_total, functools.partial(kstep, masked=True), carry)
            outs = [c[1] / pltpu.roll(c[1], HEAD_DIM, 1) for c in carry]
        o_ref[0, pl.ds(q0, tq), :] = jnp.where(lo_half, outs[0], outs[1]).astype(o_ref.dtype)
        return 0

    _loop(0, nq, qblock, 0) if nq == 1 else lax.fori_loop(0, nq, qblock, 0)


def _attention(kind, q, k, v, off, tq, tk, skp, f=None, cache=None):
    b, sq, qw = q.shape
    sn = k.shape[1]
    ql = qw // N_PAIRS
    pair = lambda n, w: pl.BlockSpec((1, n, w), lambda i, p: (i, 0, p))
    in_specs = [pair(sq, ql), pair(sn, ql), pair(sn, LANES)]
    args = [q, k, v]
    scratch = []
    if kind == "fox":
        in_specs.append(pl.BlockSpec((1, 1, skp, 2), lambda i, p: (i, p, 0, 0)))
        args.append(f)
    if cache is not None:
        ck, cv = cache
        in_specs += [pair(ck.shape[1], ql), pair(cv.shape[1], LANES)]
        args += [ck, cv]
        scratch += [pltpu.VMEM((skp, ql), BF16), pltpu.VMEM((skp, LANES), BF16)]
    else:
        assert skp == sn
    if kind == "fox":
        scratch += [pltpu.VMEM((skp, LANES), BF16)] * 2
    if kind != "sb":
        scratch += [pltpu.VMEM((skp, LANES), BF16)] * 2
    return pl.pallas_call(
        functools.partial(_attn_kernel, kind=kind, sq=sq, skp=skp, off=off, tq=tq, tk=tk,
                          has_cache=cache is not None),
        grid=(b, N_PAIRS),
        in_specs=in_specs,
        out_specs=pair(sq, LANES),
        out_shape=jax.ShapeDtypeStruct((b, sq, N_HEADS * HEAD_DIM), BF16),
        scratch_shapes=scratch,
        compiler_params=_cparams("parallel", "parallel"),
        name="attn_" + kind,
    )(*args)


def _attn_t_kernel(*refs, kind, sq, off, tq, tk):
    it = iter(refs)
    q_ref, k_ref, v_ref = next(it), next(it), next(it)
    f_ref = next(it) if kind == "fox" else None
    o_ref = next(it)
    ka = (next(it), next(it)) if kind == "fox" else None
    vt = (next(it), next(it)) if kind != "sb" else (next(it),)
    skp = k_ref.shape[1]

    lane_c = lax.broadcasted_iota(jnp.int32, (ROW_CHUNK, LANES), 1)
    chan = lax.broadcasted_iota(jnp.int32, (LANES, ROW_CHUNK), 0)

    def build(i, _):
        r0 = pl.multiple_of(i * ROW_CHUNK, ROW_CHUNK)
        vct = v_ref[0, pl.ds(r0, ROW_CHUNK), :].astype(F32).T
        if kind == "sb":
            vt[0][:, pl.ds(r0, ROW_CHUNK)] = vct.astype(BF16)
        else:
            vt[0][:, pl.ds(r0, ROW_CHUNK)] = jnp.where(chan < HEAD_DIM, vct, 1.0).astype(BF16)
            vt[1][:, pl.ds(r0, ROW_CHUNK)] = jnp.where(chan >= HEAD_DIM, vct, 1.0).astype(BF16)
        if kind == "fox":
            kc = k_ref[0, pl.ds(r0, ROW_CHUNK), :]
            for hh in range(2):
                own = (lane_c < HEAD_DIM) if hh == 0 else (lane_c >= HEAD_DIM)
                base = HEAD_DIM if hh == 0 else 0
                g1, g2, g3 = _split3(f_ref[0, 0, pl.ds(r0, ROW_CHUNK), hh:hh + 1])
                e = jnp.where(lane_c == base + 3, -g1,
                              jnp.where(lane_c == base + 4, -g2,
                                        jnp.where(lane_c == base + 5, -g3,
                                                  jnp.where((lane_c >= base) & (lane_c < base + 3),
                                                            1.0, 0.0))))
                ka[hh][pl.ds(r0, ROW_CHUNK), :] = jnp.where(own, kc, e.astype(BF16))
        return 0

    lax.fori_loop(0, skp // ROW_CHUNK, build, 0)

    nq = sq // tq
    krow = lax.broadcasted_iota(jnp.int32, (tk, tq), 0)
    qcol = lax.broadcasted_iota(jnp.int32, (tk, tq), 1)
    lane_q = lax.broadcasted_iota(jnp.int32, (tq, LANES), 1)
    lo_half = lane_q < HEAD_DIM
    out_lo = lax.broadcasted_iota(jnp.int32, (LANES, tq), 0) < HEAD_DIM
    if kind == "sb":
        a_ = lax.broadcasted_iota(jnp.int32, (LANES, 2 * LANES), 0)
        b_ = lax.broadcasted_iota(jnp.int32, (LANES, 2 * LANES), 1)
        b_ = jnp.where(b_ >= LANES, b_ - LANES, b_)
        later_t = jnp.where(b_ > a_, 1.0, 0.0).astype(BF16)

    def qblock(qi, _):
        q0 = pl.multiple_of(qi * tq, tq)
        qblk = q_ref[0, pl.ds(q0, tq), :]
        qs = []
        for hh in range(2):
            own = lo_half if hh == 0 else ~lo_half
            if kind == "mla":
                qs.append(qblk[:, hh * LANES:(hh + 1) * LANES])
            elif kind == "sb":
                qs.append(jnp.where(own, qblk, jnp.zeros_like(qblk)))
            else:
                base = HEAD_DIM if hh == 0 else 0
                f1, f2, f3 = _split3(f_ref[0, 0, pl.ds(off + q0, tq), hh:hh + 1])
                e = jnp.where(lane_q == base, f1,
                              jnp.where(lane_q == base + 1, f2,
                                        jnp.where(lane_q == base + 2, f3,
                                                  jnp.where((lane_q >= base + 3) & (lane_q < base + 6),
                                                            1.0, 0.0))))
                qs.append(jnp.where(own, qblk, e.astype(BF16)))

        def kload(hh, k0):
            if kind == "fox":
                return ka[hh][pl.ds(k0, tk), :]
            if kind == "mla":
                return k_ref[0, pl.ds(k0, tk), hh * LANES:(hh + 1) * LANES]
            return k_ref[0, pl.ds(k0, tk), :]

        if kind == "sb":
            n_full = (off + q0) // tk
            n_total = (off + q0 + tq - 1 + tk - 1) // tk

            def kstep(jj, carry, masked, base):
                j = base - jj
                k0 = pl.multiple_of(j * tk, tk)
                vb = vt[0][:, pl.ds(k0, tk)]
                new = []
                for hh in range(2):
                    rem, acc = carry[hh]
                    z = _dot_nt(kload(hh, k0), qs[hh])
                    nl = _softplus(z)
                    if masked:
                        vis = (k0 + krow) < (off + q0 + qcol)
                        nl = jnp.where(vis, nl, 0.0)
                    hi = nl.astype(BF16)
                    lo = (nl - hi.astype(F32)).astype(BF16)
                    after = rem
                    parts = []
                    for sb in reversed(range(tk // LANES)):
                        rs = slice(sb * LANES, (sb + 1) * LANES)
                        within = _dot(later_t, jnp.concatenate([hi[rs], lo[rs]], axis=0))
                        parts.append(within + after)
                        after = after + within[0:1, :] + nl[sb * LANES:sb * LANES + 1, :]
                    suf = jnp.concatenate(parts[::-1], axis=0)
                    a = jnp.exp(z - (nl + suf))
                    if masked:
                        a = jnp.where(vis, a, 0.0)
                    new.append((after, acc + _dot(vb, a.astype(BF16))))
                return tuple(new)

            init = ((jnp.zeros((1, tq), F32), jnp.zeros((LANES, tq), F32)),) * 2
            carry = lax.fori_loop(0, n_total - n_full,
                                  functools.partial(kstep, masked=True, base=n_total - 1), init)
            carry = lax.fori_loop(0, n_full,
                                  functools.partial(kstep, masked=False, base=n_full - 1), carry)
            outs = [c[1] for c in carry]
        else:
            if kind == "fox":
                n_full = (off + q0 + 1) // tk
                n_total = (off + q0 + tq + tk - 1) // tk
            else:
                n_full = ((off + q0) // CHUNK * CHUNK + CHUNK) // tk
                n_total = (((off + q0 + tq - 1) // CHUNK + 1) * CHUNK + tk - 1) // tk

            def kstep(j, carry, masked):
                k0 = pl.multiple_of(j * tk, tk)
                new = []
                for hh in range(2):
                    m, acc = carry[hh]
                    s = _dot_nt(kload(hh, k0), qs[hh])
                    if masked:
                        qpos = off + q0 + qcol
                        kpos = k0 + krow
                        vis = (kpos <= qpos) if kind == "fox" else ((kpos // CHUNK) <= (qpos // CHUNK))
                        s = jnp.where(vis, s, NEG)
                    m_new = jnp.maximum(m, jnp.max(s, axis=0, keepdims=True))
                    p = jnp.exp(s - m_new)
                    pv = _dot(vt[hh][:, pl.ds(k0, tk)], p.astype(BF16))
                    new.append((m_new, jnp.exp(m - m_new) * acc + pv))
                return tuple(new)

            init = ((jnp.full((1, tq), NEG, F32), jnp.zeros((LANES, tq), F32)),) * 2
            carry = lax.fori_loop(0, n_full, functools.partial(kstep, masked=False), init)
            carry = lax.fori_loop(n_full, n_total, functools.partial(kstep, masked=True), carry)
            outs = [carry[0][1] / carry[0][1][HEAD_DIM:HEAD_DIM + 1, :],
                    carry[1][1] / carry[1][1][0:1, :]]
        o_ref[0, pl.ds(q0, tq), :] = jnp.where(out_lo, outs[0], outs[1]).T.astype(o_ref.dtype)
        return 0

    lax.fori_loop(0, nq, qblock, 0)


def _attention_t(kind, q, k, v, off, tq, tk, f=None):
    b, sq, qw = q.shape
    skp = k.shape[1]
    ql = qw // N_PAIRS
    pair = lambda n, w: pl.BlockSpec((1, n, w), lambda i, p: (i, 0, p))
    in_specs = [pair(sq, ql), pair(skp, ql), pair(skp, LANES)]
    args = [q, k, v]
    scratch = []
    if kind == "fox":
        in_specs.append(pl.BlockSpec((1, 1, skp, 2), lambda i, p: (i, p, 0, 0)))
        args.append(f)
        scratch += [pltpu.VMEM((skp, LANES), BF16)] * 2
    scratch += [pltpu.VMEM((LANES, skp), BF16)] * (1 if kind == "sb" else 2)
    return pl.pallas_call(
        functools.partial(_attn_t_kernel, kind=kind, sq=sq, off=off, tq=tq, tk=tk),
        grid=(b, N_PAIRS),
        in_specs=in_specs,
        out_specs=pair(sq, LANES),
        out_shape=jax.ShapeDtypeStruct((b, sq, N_HEADS * HEAD_DIM), BF16),
        scratch_shapes=scratch,
        compiler_params=_cparams("parallel", "parallel"),
        name="attn_t_" + kind,
    )(*args)


SWA_TQ = 2 * CHUNK


def _swa_span(tq):
    return _round_up(tq + WINDOW, LANES)


def _swa_kernel(slope_ref, sink_ref, q_ref, k_ref, v_ref, o_ref, vta, vtb, *, sq, past, tq):
    nq = sq // tq
    kv = pl.program_id(1)
    sk = k_ref.shape[1]
    chan = lax.broadcasted_iota(jnp.int32, (LANES, ROW_CHUNK), 0)

    def build(i, _):
        r0 = pl.multiple_of(i * ROW_CHUNK, ROW_CHUNK)
        vct = v_ref[0, pl.ds(r0, ROW_CHUNK), :].astype(F32).T
        vta[:, pl.ds(r0, ROW_CHUNK)] = jnp.where(chan < HEAD_DIM, vct, 1.0).astype(BF16)
        vtb[:, pl.ds(r0, ROW_CHUNK)] = jnp.where(chan >= HEAD_DIM, vct, 1.0).astype(BF16)
        return 0

    lax.fori_loop(0, sk // ROW_CHUNK, build, 0)

    span = _swa_span(tq)
    krow = lax.broadcasted_iota(jnp.int32, (span, tq), 0)
    qcol = lax.broadcasted_iota(jnp.int32, (span, tq), 1)
    lo_half = _half_mask((tq, LANES))
    out_lo = lax.broadcasted_iota(jnp.int32, (LANES, tq), 0) < HEAD_DIM

    def qblock(qi, _):
        q0 = qi * tq if isinstance(qi, int) else pl.multiple_of(qi * tq, tq)
        ks = jnp.maximum(q0 + past - WINDOW, 0)
        ks = ks if isinstance(ks, int) else pl.multiple_of(ks, LANES)
        kwin = k_ref[0, pl.ds(ks, span), :]
        qpos = q0 + qcol
        kpos = ks - past + krow
        qc = qpos // CHUNK
        vis = (kpos >= qc * CHUNK - WINDOW) & (kpos < (qc + 1) * CHUNK)
        dist = jnp.abs(qpos - kpos).astype(F32)
        for pair in range(SWA_GROUPS // 2):
            qpair = q_ref[0, pl.ds(q0, tq), pair * LANES:(pair + 1) * LANES]
            outs = []
            for half in range(2):
                head = kv * SWA_GROUPS + pair * 2 + half
                qh = jnp.where(lo_half if half == 0 else ~lo_half, qpair, jnp.zeros_like(qpair))
                s = _dot_nt(kwin, qh) - slope_ref[head] * dist
                s = jnp.where(vis, s, NEG)
                sink = sink_ref[head]
                m = jnp.maximum(jnp.max(s, axis=0, keepdims=True), sink)
                e = jnp.exp(s - m).astype(BF16)
                vt = vta if half == 0 else vtb
                ot = _dot(vt[:, pl.ds(ks, span)], e)
                den = (ot[HEAD_DIM:HEAD_DIM + 1, :] if half == 0 else ot[0:1, :]) + jnp.exp(sink - m)
                outs.append(ot / den)
            o_ref[0, pl.ds(q0, tq), pair * LANES:(pair + 1) * LANES] = (
                jnp.where(out_lo, outs[0], outs[1]).T.astype(o_ref.dtype))
        return 0

    _loop(0, nq, qblock, 0) if nq == 1 else lax.fori_loop(0, nq, qblock, 0)


def _swa_attn(q, kdup, vdup, slopes, sinks, past, tq):
    b, sq, _ = q.shape
    sk = kdup.shape[1]
    gw = SWA_GROUPS * HEAD_DIM
    smem = pl.BlockSpec(memory_space=pltpu.SMEM)
    return pl.pallas_call(
        functools.partial(_swa_kernel, sq=sq, past=past, tq=tq),
        grid=(b, SWA_KV_HEADS),
        in_specs=[smem, smem,
                  pl.BlockSpec((1, sq, gw), lambda i, p: (i, 0, p)),
                  pl.BlockSpec((1, sk, LANES), lambda i, p: (i, 0, p)),
                  pl.BlockSpec((1, sk, LANES), lambda i, p: (i, 0, p))],
        out_specs=pl.BlockSpec((1, sq, gw), lambda i, p: (i, 0, p)),
        out_shape=jax.ShapeDtypeStruct((b, sq, N_HEADS * HEAD_DIM), BF16),
        scratch_shapes=[pltpu.VMEM((LANES, sk), BF16)] * 2,
        compiler_params=_cparams("parallel", "parallel"),
        name="attn_swa",
    )(slopes, sinks, q, kdup, vdup)


MLA_IN_COLS = MLA_Q_RANK + MLA_KV_RANK + 2 * LANES
MLA_HEAD_W = LANES


def _rms_norm(x, g):
    return x * lax.rsqrt(jnp.mean(x * x, axis=-1, keepdims=True) + 1e-6) * g


def _mla_proj_kernel(x_ref, sc_ref, sh_ref, win_ref, gq_ref, gkv_ref, wqa_ref, wqb_ref,
                     wk_ref, wv_ref, qc_ref, qs_ref, kc_ref, ks_ref,
                     q_ref, kf_ref, v_ref, ckv_ref, kr_ref):
    h = (x_ref[0] * (1.0 + sc_ref[0]) + sh_ref[0]).astype(BF16)
    proj = _dot(h, win_ref[...])
    cq = _rms_norm(proj[:, :MLA_Q_RANK], gq_ref[...]).astype(BF16)
    ckv = _rms_norm(proj[:, MLA_Q_RANK:MLA_Q_RANK + MLA_KV_RANK], gkv_ref[...])
    ckv_ref[0] = ckv
    ckv_b = ckv.astype(BF16)
    kr0 = MLA_Q_RANK + MLA_KV_RANK
    krr = proj[:, kr0:kr0 + LANES] * kc_ref[...] + proj[:, kr0 + LANES:kr0 + 2 * LANES] * ks_ref[...]
    kr_ref[0] = pltpu.roll(krr, LANES - MLA_D_NOPE, 1)[:, :MLA_D_ROPE]
    qa = _dot(cq, wqa_ref[...])
    qb = _dot(cq, wqb_ref[...])
    kn = _dot(ckv_b, wk_ref[...])
    v_ref[0] = _dot(ckv_b, wv_ref[...]).astype(BF16)
    qc, qs = qc_ref[...], qs_ref[...]
    for hd in range(N_HEADS):
        sl = slice(hd * MLA_HEAD_W, (hd + 1) * MLA_HEAD_W)
        q_ref[0, :, sl] = (qa[:, sl] * qc + qb[:, sl] * qs).astype(BF16)
        kf_ref[0, :, sl] = (kn[:, sl] + krr).astype(BF16)


def _mla_proj(x, sc, sh, mw, tables, tm=256):
    b, s, d = x.shape
    tm = _row_tile(s, tm)
    qc, qs, kc, ks = tables
    full = lambda a: pl.BlockSpec(a.shape, lambda i, j: (0,) * a.ndim)
    tab = pl.BlockSpec((tm, LANES), lambda i, j: (j, 0))
    row = lambda n: pl.BlockSpec((1, tm, n), lambda i, j: (i, j, 0))
    hw = N_HEADS * MLA_HEAD_W
    return pl.pallas_call(
        _mla_proj_kernel,
        grid=(b, s // tm),
        in_specs=[row(d),
                  pl.BlockSpec((1, 1, d), lambda i, j: (i, 0, 0)),
                  pl.BlockSpec((1, 1, d), lambda i, j: (i, 0, 0)),
                  full(mw["win"]), full(mw["gq"]), full(mw["gkv"]), full(mw["wqa"]),
                  full(mw["wqb"]), full(mw["wk"]), full(mw["wv"]), tab, tab, tab, tab],
        out_specs=[row(hw), row(hw), row(N_HEADS * HEAD_DIM), row(MLA_KV_RANK), row(MLA_D_ROPE)],
        out_shape=[jax.ShapeDtypeStruct((b, s, hw), BF16),
                   jax.ShapeDtypeStruct((b, s, hw), BF16),
                   jax.ShapeDtypeStruct((b, s, N_HEADS * HEAD_DIM), BF16),
                   jax.ShapeDtypeStruct((b, s, MLA_KV_RANK), F32),
                   jax.ShapeDtypeStruct((b, s, MLA_D_ROPE), F32)],
        compiler_params=_cparams("parallel", "parallel"),
        name="proj_mla",
    )(x, sc, sh, mw["win"], mw["gq"], mw["gkv"], mw["wqa"], mw["wqb"], mw["wk"], mw["wv"],
      qc, qs, kc, ks)


def _mla_cache_kernel(ckv_ref, kr_ref, wk_ref, wv_ref, place_ref, kf_ref, v_ref):
    ckv_b = ckv_ref[0].astype(BF16)
    kf_ref[0] = (_dot(ckv_b, wk_ref[...]) + _dot(kr_ref[0], place_ref[...])).astype(BF16)
    v_ref[0] = _dot(ckv_b, wv_ref[...]).astype(BF16)


def _mla_cache_up(ckv, kr_pad, mw, tm=512):
    b, p, r = ckv.shape
    tm = _row_tile(p, tm)
    hw = N_HEADS * MLA_HEAD_W
    full = lambda a: pl.BlockSpec(a.shape, lambda i, j: (0,) * a.ndim)
    row = lambda n: pl.BlockSpec((1, tm, n), lambda i, j: (i, j, 0))
    return pl.pallas_call(
        _mla_cache_kernel,
        grid=(b, p // tm),
        in_specs=[row(r), row(LANES), full(mw["wk"]), full(mw["wv"]), full(mw["place"])],
        out_specs=[row(hw), row(N_HEADS * HEAD_DIM)],
        out_shape=[jax.ShapeDtypeStruct((b, p, hw), BF16),
                   jax.ShapeDtypeStruct((b, p, N_HEADS * HEAD_DIM), BF16)],
        compiler_params=_cparams("parallel", "parallel"),
        name="mla_cache_up",
    )(ckv, kr_pad, mw["wk"], mw["wv"], mw["place"])


def _xor_partner(x, k, idx, axis):
    n = x.shape[axis]
    up = pltpu.roll(x, n - k, axis)
    dn = pltpu.roll(x, k, axis)
    return jnp.where((idx & k) == 0, up, dn)


def _argmax_groups(v, idx, pos, strides, axis):
    for k in strides:
        pv = _xor_partner(v, k, pos, axis)
        pi = _xor_partner(idx, k, pos, axis)
        take = (pv > v) | ((pv == v) & (pi < idx))
        v = jnp.where(take, pv, v)
        idx = jnp.where(take, pi, idx)
    return v, idx


def _route(logits, rb):
    ex = lax.broadcasted_iota(jnp.int32, logits.shape, 0)
    scores = _sigmoid(logits)
    biased = scores + rb
    p1 = _xor_partner(biased, 1, ex, 0)
    hi1, lo1 = jnp.maximum(biased, p1), jnp.minimum(biased, p1)
    hi2, lo2 = _xor_partner(hi1, 2, ex, 0), _xor_partner(lo1, 2, ex, 0)
    group_score = jnp.maximum(hi1, hi2) + jnp.maximum(jnp.minimum(hi1, hi2), jnp.maximum(lo1, lo2))
    gid = ex >> 2
    _, best = _argmax_groups(group_score, gid, ex, (4, 8), 0)
    cand = jnp.where(gid == best, biased, NEG)
    _, i1 = _argmax_groups(cand, ex, ex, (1, 2, 4, 8), 0)
    _, i2 = _argmax_groups(jnp.where(ex == i1, -jnp.inf, cand), ex, ex, (1, 2, 4, 8), 0)
    sel = jnp.where((ex == i1) | (ex == i2), scores, 0.0)
    return sel / jnp.sum(sel, axis=0, keepdims=True)


def _outproj_kernel(o_ref, x_ref, w_ref, ga_ref, scf_ref, shf_ref, g_ref, b_ref,
                    rwh_ref, rwl_ref, rb_ref, x1_ref, h2_ref, gate_ref):
    y = DEEPNORM_ALPHA * x_ref[0] + (1.0 + ga_ref[0]) * _dot(o_ref[0], w_ref[...])
    x1 = _layer_norm(y, g_ref[...], b_ref[...])
    x1_ref[0] = x1
    h2 = x1 * (1.0 + scf_ref[0]) + shf_ref[0]
    hi = h2.astype(BF16)
    h2_ref[0] = hi
    lo = (h2 - hi.astype(F32)).astype(BF16)
    logits = _dot_nt(rwh_ref[...], hi) + _dot_nt(rwh_ref[...], lo) + _dot_nt(rwl_ref[...], hi)
    gates = _route(logits, rb_ref[...])
    pad = jnp.zeros((LANES - N_EXPERTS, gates.shape[1]), F32)
    gate_ref[0] = jnp.concatenate([gates, pad], axis=0).T[:, :N_EXPERTS]


def _outproj(o, x, w, ga, scf, shf, g, bta, rwh, rwl, rb, tm=512):
    b, s, d = x.shape
    tm = _row_tile(s, tm)
    row = lambda n: pl.BlockSpec((1, tm, n), lambda i, j: (i, j, 0))
    mod = pl.BlockSpec((1, 1, d), lambda i, j: (i, 0, 0))
    full = lambda a: pl.BlockSpec(a.shape, lambda i, j: (0,) * a.ndim)
    return pl.pallas_call(
        _outproj_kernel,
        grid=(b, s // tm),
        in_specs=[row(d), row(d), full(w), mod, mod, mod, full(g), full(bta),
                  full(rwh), full(rwl), full(rb)],
        out_specs=[row(d), row(d), row(N_EXPERTS)],
        out_shape=[jax.ShapeDtypeStruct((b, s, d), F32), jax.ShapeDtypeStruct((b, s, d), BF16),
                   jax.ShapeDtypeStruct((b, s, N_EXPERTS), F32)],
        compiler_params=_cparams("parallel", "parallel"),
        name="outproj_ln_router",
    )(o, x, w, ga, scf, shf, g, bta, rwh, rwl, rb)


EXPERT_UNROLL = 4


def _moe_kernel(h_ref, gate_ref, x_ref, gf_ref, g_ref, b_ref, wg_ref, wu_ref, wd_ref,
                o_ref, acc_ref):
    bt, tm, d = h_ref.shape
    rows = bt * tm
    ne, _, de = wg_ref.shape
    h = h_ref[...].reshape(rows, d)
    gates = gate_ref[...].reshape(rows, N_EXPERTS)
    lane = lax.broadcasted_iota(jnp.int32, gates.shape, 1)
    acc_ref[...] = jnp.zeros_like(acc_ref)

    def body(i, _):
        e0 = i * EXPERT_UNROLL
        acts = []
        for u in range(EXPERT_UNROLL):
            e = e0 + u
            hg = _dot(h, wg_ref[e])
            hu = _dot(h, wu_ref[e])
            ge = jnp.sum(jnp.where(lane == e, gates, 0.0), axis=-1, keepdims=True)
            acts.append((hg * _sigmoid(hg) * hu * ge).astype(BF16))
        wd = wd_ref[pl.ds(e0, EXPERT_UNROLL)].reshape(EXPERT_UNROLL * de, d)
        acc_ref[...] += _dot(jnp.concatenate(acts, axis=1), wd)
        return 0

    lax.fori_loop(0, ne // EXPERT_UNROLL, body, 0)
    y = DEEPNORM_ALPHA * x_ref[...] + (1.0 + gf_ref[...]) * acc_ref[...].reshape(bt, tm, d)
    o_ref[...] = _layer_norm(y, g_ref[...], b_ref[...])


def _moe(h2, gates, x1, gf, g, bta, wg, wu, wd, rows=512):
    b, s, d = x1.shape
    tm = _row_tile(s, rows)
    bt = min(b, rows // tm)
    row = lambda n: pl.BlockSpec((bt, tm, n), lambda i, j: (i, j, 0))
    full = lambda a: pl.BlockSpec(a.shape, lambda i, j: (0,) * a.ndim)
    once = lambda a: pl.BlockSpec(a.shape, lambda i, j: (0,) * a.ndim, pipeline_mode=pl.Buffered(1))
    return pl.pallas_call(
        _moe_kernel,
        grid=(b // bt, s // tm),
        in_specs=[row(d), row(N_EXPERTS), row(d),
                  pl.BlockSpec((bt, 1, d), lambda i, j: (i, 0, 0)), full(g), full(bta),
                  once(wg), once(wu), once(wd)],
        out_specs=row(d),
        out_shape=jax.ShapeDtypeStruct((b, s, d), F32),
        scratch_shapes=[pltpu.VMEM((bt * tm, d), F32)],
        compiler_params=_cparams("parallel", "parallel"),
        name="moe_ffn",
    )(h2, gates, x1, gf, g, bta, wg, wu, wd)


def _pad_cols(w, n):
    return jnp.pad(w, ((0, 0), (0, n - w.shape[1])))


def _rot_half_cols(w):
    half = w.shape[-1] // 2
    return jnp.concatenate([-w[..., half:], w[..., :half]], axis=-1)


def _mla_weights(mla_w_in, mla_q_norm, mla_w_uq, mla_kv_norm, mla_w_uk, mla_w_uv):
    d = mla_w_in.shape[0]
    kr0 = MLA_Q_RANK + MLA_KV_RANK
    wkr = mla_w_in[:, kr0:]
    z = lambda n: jnp.zeros((d, n), F32)
    win = jnp.concatenate([mla_w_in[:, :kr0], z(MLA_D_NOPE), wkr, z(LANES - MLA_D_NOPE - MLA_D_ROPE),
                           z(MLA_D_NOPE), _rot_half_cols(wkr), z(LANES - MLA_D_NOPE - MLA_D_ROPE)],
                          axis=1)
    wuq = mla_w_uq.reshape(MLA_Q_RANK, N_HEADS, MLA_D_NOPE + MLA_D_ROPE)
    zq = lambda n: jnp.zeros((MLA_Q_RANK, N_HEADS, n), F32)
    tail = LANES - MLA_D_NOPE - MLA_D_ROPE
    wqa = jnp.concatenate([wuq, zq(tail)], axis=-1).reshape(MLA_Q_RANK, N_HEADS * LANES)
    wqb = jnp.concatenate([zq(MLA_D_NOPE), _rot_half_cols(wuq[..., MLA_D_NOPE:]), zq(tail)],
                          axis=-1).reshape(MLA_Q_RANK, N_HEADS * LANES)
    wk = jnp.concatenate([mla_w_uk, jnp.zeros((MLA_KV_RANK, N_HEADS, LANES - MLA_D_NOPE), F32)],
                         axis=-1).reshape(MLA_KV_RANK, N_HEADS * LANES)
    wv = mla_w_uv.reshape(MLA_KV_RANK, N_HEADS * HEAD_DIM)
    src = jnp.arange(LANES)[:, None]
    dst = jnp.arange(N_HEADS * LANES)[None, :]
    place = ((dst % LANES) == src + MLA_D_NOPE) & (src < MLA_D_ROPE)
    return dict(win=win.astype(BF16), gq=mla_q_norm.reshape(1, -1), gkv=mla_kv_norm.reshape(1, -1),
                wqa=wqa.astype(BF16), wqb=wqb.astype(BF16), wk=wk.astype(BF16),
                wv=wv.astype(BF16), place=place.astype(BF16))


def _rope_tables(n_pos):
    half = MLA_D_ROPE // 2
    inv = ROPE_THETA ** (-jnp.arange(half, dtype=F32) * 2.0 / MLA_D_ROPE)
    ang = jnp.arange(n_pos, dtype=F32)[:, None] * inv[None, :]
    cos = jnp.concatenate([jnp.cos(ang), jnp.cos(ang)], axis=-1)
    sin = jnp.concatenate([jnp.sin(ang), jnp.sin(ang)], axis=-1)
    scale = (MLA_D_NOPE + MLA_D_ROPE) ** -0.5
    z = lambda n: jnp.zeros((n_pos, n), F32)
    tail = LANES - MLA_D_NOPE - MLA_D_ROPE
    qc = jnp.concatenate([jnp.full((n_pos, MLA_D_NOPE), scale, F32), cos * scale, z(tail)], axis=1)
    qs = jnp.concatenate([z(MLA_D_NOPE), sin * scale, z(tail)], axis=1)
    kc = jnp.concatenate([z(MLA_D_NOPE), cos, z(tail)], axis=1)
    ks = jnp.concatenate([z(MLA_D_NOPE), sin, z(tail)], axis=1)
    return qc, qs, kc, ks


def _dup_heads(w):
    d = w.shape[0]
    w4 = w.reshape(d, SWA_KV_HEADS, HEAD_DIM)
    return jnp.concatenate([w4, w4], axis=-1).reshape(d, SWA_KV_HEADS * LANES)


def _pad_rows(a, n):
    return jnp.pad(a, ((0, 0), (0, n - a.shape[1]), (0, 0)))


def _round_up(n, m):
    return -(-n // m) * m


def _attn_tiles(kind, sq, n_keys):
    if sq % LONG_TQ == 0:
        return LONG_TQ, LONG_TQ, n_keys
    if kind == "sb":
        return sq, 256, _round_up(n_keys, 256)
    skp = _round_up(n_keys, LANES)
    return sq, skp, skp


def _flat_heads(a):
    return a.reshape(a.shape[0], a.shape[1], -1)


def _fox_mixer(x, sc, sh, fw, cache):
    b, s, _ = x.shape
    q, k32, kb, v32, vb, logf = _proj(x, sc, sh, fw["w"], [[BF16], [F32, BF16], [F32, BF16], ["logf"]],
                                      bias=fw["bias"], name="proj_fox")
    off = 0 if cache is None else cache[0].shape[1]
    tq, tk, skp = _attn_tiles("fox", s, off + s)
    lall = logf if cache is None else jnp.concatenate([cache[2], logf], axis=1)
    f = _forget_cumsum(_pad_rows(lall, skp))
    f = f.reshape(b, skp, N_PAIRS, 2).transpose(0, 2, 1, 3)
    kv_cache = None if cache is None else (_flat_heads(cache[0]), _flat_heads(cache[1]))
    if cache is None:
        o = _attention_t("fox", q, kb, vb, off, tq, tk, f=f)
    else:
        o = _attention("fox", q, kb, vb, off, tq, tk, skp, f=f, cache=kv_cache)
    hshape = (b, s, N_HEADS, HEAD_DIM)
    return o, (k32.reshape(hshape), v32.reshape(hshape), logf)


def _mla_mixer(x, sc, sh, mw, cache):
    b, s, _ = x.shape
    off = 0 if cache is None else cache[0].shape[1]
    tables = [t[off:off + s] for t in mw["tables"]]
    q, kf, v, ckv, kr = _mla_proj(x, sc, sh, mw, tables)
    tq, tk, skp = _attn_tiles("mla", s, off + s)
    kv_cache = None
    if cache is not None:
        c_ckv, c_kr = cache
        kr_pad = jnp.pad(c_kr, ((0, 0), (0, 0), (0, LANES - MLA_D_ROPE))).astype(BF16)
        kv_cache = _mla_cache_up(c_ckv, kr_pad, mw)
    if cache is None:
        o = _attention_t("mla", q, kf, v, off, tq, tk)
    else:
        o = _attention("mla", q, kf, v, off, tq, tk, skp, cache=kv_cache)
    return o, (ckv, kr)


def _swa_mixer(x, sc, sh, sw, cache):
    b, s, _ = x.shape
    q, k32, kdup, v32, vdup = _proj(x, sc, sh, sw["w"], [[BF16], [F32], [BF16], [F32], [BF16]],
                                    name="proj_swa")
    kvshape = lambda a: a.reshape(a.shape[0], a.shape[1], SWA_KV_HEADS, HEAD_DIM)
    if cache is None:
        past, tq = 0, min(s, SWA_TQ)
        new_k, new_v = kvshape(k32[:, -WINDOW:]), kvshape(v32[:, -WINDOW:])
    else:
        ck, cv = cache
        past, tq = ck.shape[1], s
        dup = lambda c: jnp.concatenate([c, c], axis=-1).reshape(c.shape[0], c.shape[1], -1).astype(BF16)
        kdup = _pad_rows(jnp.concatenate([dup(ck), kdup], axis=1), _swa_span(tq))
        vdup = _pad_rows(jnp.concatenate([dup(cv), vdup], axis=1), _swa_span(tq))
        new_k = jnp.concatenate([ck, kvshape(k32)], axis=1)[:, -WINDOW:]
        new_v = jnp.concatenate([cv, kvshape(v32)], axis=1)[:, -WINDOW:]
    o = _swa_attn(q, kdup, vdup, sw["slopes"], sw["sinks"], past, tq)
    return o, (new_k, new_v)


def _sb_mixer(x, sc, sh, bw, cache):
    b, s, _ = x.shape
    q, k32, kb, v32, vb = _proj(x, sc, sh, bw["w"], [[BF16], [F32, BF16], [F32, BF16]], name="proj_sb")
    off = 0 if cache is None else cache[0].shape[1]
    tq, tk, skp = _attn_tiles("sb", s, off + s)
    kv_cache = None if cache is None else (_flat_heads(cache[0]), _flat_heads(cache[1]))
    if cache is None:
        o = _attention_t("sb", q, kb, vb, off, tq, tk)
    else:
        o = _attention("sb", q, kb, vb, off, tq, tk, skp, cache=kv_cache)
    hshape = (b, s, N_HEADS, HEAD_DIM)
    return o, (k32.reshape(hshape), v32.reshape(hshape))


_MIXERS = (_fox_mixer, _mla_mixer, _swa_mixer, _sb_mixer)


def _trunk(x, mods, caches, wts):
    states = []
    for i in range(DEPTH):
        m = [mods[i, :, j][:, None, :] for j in range(6)]
        sh_a, sc_a, g_a, sh_f, sc_f, g_f = m
        o, st = _MIXERS[i](x, sc_a, sh_a, wts["mixer"][i], caches[i])
        states.extend(st)
        x1, h2, gates = _outproj(o, x, wts["w_out"][i], g_a, sc_f, sh_f,
                                 wts["ln_g"][i, 0:1], wts["ln_b"][i, 0:1],
                                 wts["rw_hi"], wts["rw_lo"], wts["rb"])
        x = _moe(h2, gates, x1, g_f, wts["ln_g"][i, 1:2], wts["ln_b"][i, 1:2],
                 wts["moe_g"][i], wts["moe_u"][i], wts["moe_d"][i])
    return x, states


def kernel(x_prompt, x_sample, cache_fox_k, cache_fox_v, cache_fox_logf, cache_mla_ckv, cache_mla_krope, cache_swa_k, cache_swa_v, cache_sb_k, cache_sb_v, c_prompt, c_sample, ada_w, ada_b, ln_g, ln_b, fox_w_in, fox_b_f, fox_w_out, mla_w_in, mla_q_norm, mla_w_uq, mla_kv_norm, mla_w_uk, mla_w_uv, mla_w_out, swa_w_in, swa_sinks, swa_w_out, sb_w_in, sb_w_out, router_w, router_b, moe_w_gate, moe_w_up, moe_w_down):
    bp, sp, d = x_prompt.shape
    bs, ss, _ = x_sample.shape
    past = cache_fox_k.shape[1]
    hw = N_HEADS * HEAD_DIM
    qscale = HEAD_DIM ** -0.5

    fox = dict(w=[(fox_w_in[:, :hw] * qscale).astype(BF16), fox_w_in[:, hw:2 * hw].astype(BF16),
                  fox_w_in[:, 2 * hw:3 * hw].astype(BF16), _pad_cols(fox_w_in[:, 3 * hw:], LANES).astype(BF16)],
               bias=_pad_cols(fox_b_f.reshape(1, -1), LANES))
    mla = _mla_weights(mla_w_in, mla_q_norm, mla_w_uq, mla_kv_norm, mla_w_uk, mla_w_uv)
    mla["tables"] = _rope_tables(max(sp, past + ss))
    kvw = SWA_KV_HEADS * HEAD_DIM
    wk, wv = swa_w_in[:, hw:hw + kvw], swa_w_in[:, hw + kvw:]
    swa = dict(w=[(swa_w_in[:, :hw] * qscale).astype(BF16), wk.astype(BF16), _dup_heads(wk).astype(BF16),
                  wv.astype(BF16), _dup_heads(wv).astype(BF16)],
               slopes=jnp.exp2(-8.0 * jnp.arange(1, N_HEADS + 1, dtype=F32) / N_HEADS),
               sinks=swa_sinks.astype(F32))
    sb = dict(w=[(sb_w_in[:, :hw] * qscale).astype(BF16), sb_w_in[:, hw:2 * hw].astype(BF16),
                 sb_w_in[:, 2 * hw:].astype(BF16)])
    rw = router_w.T
    rw_hi = rw.astype(BF16)
    wts = dict(mixer=(fox, mla, swa, sb),
               w_out=[w.astype(BF16) for w in (fox_w_out, mla_w_out, swa_w_out, sb_w_out)],
               ln_g=ln_g, ln_b=ln_b, rw_hi=rw_hi, rw_lo=(rw - rw_hi.astype(F32)).astype(BF16),
               rb=router_b.reshape(-1, 1),
               moe_g=moe_w_gate.astype(BF16), moe_u=moe_w_up.astype(BF16), moe_d=moe_w_down.astype(BF16))

    mods = _ada_mod(jnp.concatenate([c_prompt, c_sample], axis=0), ada_w, ada_b)
    mods_p = mods[:, :bp].reshape(DEPTH, bp, 6, d)
    mods_s = mods[:, bp:].reshape(DEPTH, bs, 6, d)

    y_p, st_p = _trunk(x_prompt, mods_p, (None, None, None, None), wts)
    caches = ((cache_fox_k, cache_fox_v, cache_fox_logf), (cache_mla_ckv, cache_mla_krope),
              (cache_swa_k, cache_swa_v), (cache_sb_k, cache_sb_v))
    y_s, st_s = _trunk(x_sample, mods_s, caches, wts)
    fox_k_p, fox_v_p, fox_logf_p, mla_ckv_p, mla_krope_p, swa_k_p, swa_v_p, sb_k_p, sb_v_p = st_p
    fox_k_s, fox_v_s, fox_logf_s, mla_ckv_s, mla_krope_s, swa_k_s, swa_v_s, sb_k_s, sb_v_s = st_s
    return (y_p, y_s, fox_k_p, fox_k_s, fox_v_p, fox_v_s, fox_logf_p, fox_logf_s,
            mla_ckv_p, mla_ckv_s, mla_krope_p, mla_krope_s, swa_k_p, swa_k_s, swa_v_p, swa_v_s,
            sb_k_p, sb_k_s, sb_v_p, sb_v_s)
```

```python
import functools
import math

import jax
import jax.numpy as jnp
from jax import lax
from jax.experimental import pallas as pl
from jax.experimental.pallas import tpu as pltpu

F32 = jnp.float32
BF16 = jnp.bfloat16

D_MODEL = 1024
DEPTH = 4
CHUNK = 64
HEAD_DIM = 64
N_HEADS = 16
N_PAIRS = N_HEADS // 2
LANES = 128
MLA_Q_RANK = 384
MLA_KV_RANK = 256
MLA_D_NOPE = 64
MLA_D_ROPE = 32
ROPE_THETA = 10000.0
SWA_KV_HEADS = 4
SWA_GROUPS = 4
WINDOW = 128
N_EXPERTS = 16
D_EXPERT = 256
DEEPNORM_ALPHA = (2.0 * DEPTH) ** 0.25
NEG = -1e30
LOG2E = 1.4426950408889634
LONG_TQ = 512
VMEM_LIMIT = 56 * 1024 * 1024


def _cparams(*sem):
    return pltpu.CompilerParams(dimension_semantics=sem, vmem_limit_bytes=VMEM_LIMIT)


def _dot(a, b):
    return jnp.dot(a, b, preferred_element_type=F32)


def _dot_nt(a, b):
    return lax.dot_general(a, b, (((1,), (1,)), ((), ())), preferred_element_type=F32)


def _dot_f32(a, b):
    return lax.dot_general(a, b, (((1,), (0,)), ((), ())), precision=lax.Precision.HIGHEST,
                           preferred_element_type=F32)


def _sigmoid(x):
    return 1.0 / (1.0 + jnp.exp(-x))


def _log_sigmoid(x):
    return jnp.minimum(x, 0.0) - jnp.log1p(jnp.exp(-jnp.abs(x)))


def _softplus(x):
    return jnp.maximum(x, 0.0) + jnp.log(1.0 + jnp.exp2(jnp.abs(x) * -LOG2E))


def _layer_norm(y, g, b):
    mu = jnp.mean(y, axis=-1, keepdims=True)
    yc = y - mu
    var = jnp.mean(yc * yc, axis=-1, keepdims=True)
    return yc * lax.rsqrt(var + 1e-5) * g + b


def _ada_kernel(c_ref, w_ref, b_ref, o_ref):
    c = c_ref[...]
    o_ref[0] = _dot_f32(c * _sigmoid(c), w_ref[0]) + b_ref[0]


def _ada_mod(c, ada_w, ada_b):
    n, d = c.shape
    depth, _, n6 = ada_w.shape
    tn = 1536
    return pl.pallas_call(
        _ada_kernel,
        grid=(depth, n6 // tn),
        in_specs=[pl.BlockSpec((n, d), lambda i, j: (0, 0)),
                  pl.BlockSpec((1, d, tn), lambda i, j: (i, 0, j)),
                  pl.BlockSpec((1, 1, tn), lambda i, j: (i, 0, j))],
        out_specs=pl.BlockSpec((1, n, tn), lambda i, j: (i, 0, j)),
        out_shape=jax.ShapeDtypeStruct((depth, n, n6), F32),
        compiler_params=_cparams("parallel", "parallel"),
        name="ada_mod",
    )(c, ada_w, ada_b.reshape(depth, 1, n6))


def _proj_kernel(*refs, n_w, emits, has_bias):
    x_ref, sc_ref, sh_ref = refs[:3]
    w_refs = refs[3:3 + n_w]
    pos = 3 + n_w
    bias_ref = refs[pos] if has_bias else None
    o_refs = refs[pos + (1 if has_bias else 0):]
    h = (x_ref[0] * (1.0 + sc_ref[0]) + sh_ref[0]).astype(BF16)
    oi = 0
    for j in range(n_w):
        y = _dot(h, w_refs[j][...])
        for kind in emits[j]:
            if kind == "logf":
                o_refs[oi][0] = _log_sigmoid(y + bias_ref[...])[:, :N_HEADS]
            else:
                o_refs[oi][0] = y.astype(kind)
            oi += 1


def _row_tile(s, target):
    return min(s, target)


def _proj(x, sc, sh, weights, emits, bias=None, tm=512, name="proj"):
    b, s, d = x.shape
    tm = _row_tile(s, tm)
    in_specs = [pl.BlockSpec((1, tm, d), lambda i, j: (i, j, 0)),
                pl.BlockSpec((1, 1, d), lambda i, j: (i, 0, 0)),
                pl.BlockSpec((1, 1, d), lambda i, j: (i, 0, 0))]
    args = [x, sc, sh]
    for w in weights:
        in_specs.append(pl.BlockSpec(w.shape, lambda i, j: (0, 0)))
        args.append(w)
    if bias is not None:
        in_specs.append(pl.BlockSpec(bias.shape, lambda i, j: (0, 0)))
        args.append(bias)
    out_specs, out_shape = [], []
    for w, em in zip(weights, emits):
        for kind in em:
            n, dt = (N_HEADS, F32) if kind == "logf" else (w.shape[1], kind)
            out_specs.append(pl.BlockSpec((1, tm, n), lambda i, j: (i, j, 0)))
            out_shape.append(jax.ShapeDtypeStruct((b, s, n), dt))
    return pl.pallas_call(
        functools.partial(_proj_kernel, n_w=len(weights), emits=emits, has_bias=bias is not None),
        grid=(b, s // tm),
        in_specs=in_specs, out_specs=out_specs, out_shape=out_shape,
        compiler_params=_cparams("parallel", "parallel"),
        name=name,
    )(*args)


FCOEF = LANES // N_HEADS


def _cumsum_kernel(x_ref, e_ref, pad_ref):
    nblk = x_ref.shape[1] // LANES
    r = lax.broadcasted_iota(jnp.int32, (LANES, LANES), 0)
    c = lax.broadcasted_iota(jnp.int32, (LANES, LANES), 1)
    tri = (c <= r).astype(F32)
    spread = [jnp.where(c == r * FCOEF + i, 1.0, 0.0).astype(BF16) for i in range(3)]
    ones = jnp.where((c % FCOEF) == 3, 1.0, 0.0)[0:1, :]
    pad_ref[...] = jnp.zeros_like(pad_ref)

    def body(i, carry):
        r0 = pl.multiple_of(i * LANES, LANES)
        pad_ref[:, :N_HEADS] = x_ref[0, pl.ds(r0, LANES), :]
        f = _dot_f32(tri, pad_ref[...]) + carry
        parts = _split3(f)
        e = ones + sum(_dot(parts[i].astype(BF16), spread[i]) for i in range(3))
        e_ref[0, pl.ds(r0, LANES), :] = e.astype(BF16)
        return f[LANES - 1:LANES, :]

    lax.fori_loop(0, nblk, body, jnp.zeros((1, LANES), F32))


def _forget_cumsum(logf):
    b, s, h = logf.shape
    return pl.pallas_call(
        _cumsum_kernel,
        grid=(b,),
        in_specs=[pl.BlockSpec((1, s, h), lambda i: (i, 0, 0))],
        out_specs=pl.BlockSpec((1, s, LANES), lambda i: (i, 0, 0)),
        out_shape=jax.ShapeDtypeStruct((b, s, LANES), BF16),
        scratch_shapes=[pltpu.VMEM((LANES, LANES), F32)],
        compiler_params=_cparams("parallel"),
        name="forget_cumsum",
    )(logf)


ROW_CHUNK = 128


def _half_mask(shape):
    return lax.broadcasted_iota(jnp.int32, shape, 1) < HEAD_DIM


def _loop(lo, hi, body, carry):
    if isinstance(lo, int) and isinstance(hi, int):
        for i in range(lo, hi):
            carry = body(i, carry)
        return carry
    return lax.fori_loop(lo, hi, body, carry)


def _split3(f):
    f1 = f.astype(BF16).astype(F32)
    r1 = f - f1
    f2 = r1.astype(BF16).astype(F32)
    return f1, f2, r1 - f2


def _forget_placement(pair):
    r = lax.broadcasted_iota(jnp.int32, (LANES, LANES), 0)
    c = lax.broadcasted_iota(jnp.int32, (LANES, LANES), 1)
    kmat = jnp.zeros((LANES, LANES), F32)
    qmat = jnp.zeros((LANES, LANES), F32)
    for hh in range(2):
        src = (pair * 2 + hh) * FCOEF
        base = HEAD_DIM if hh == 0 else 0
        i = r - src
        coef = (i >= 0) & (i < 3)
        one = i == 3
        kmat = kmat + jnp.where(coef & (c == base + 3 + i), -1.0, 0.0)
        kmat = kmat + jnp.where(one & (c >= base) & (c < base + 3), 1.0, 0.0)
        qmat = qmat + jnp.where(coef & (c == base + i), 1.0, 0.0)
        qmat = qmat + jnp.where(one & (c >= base + 3) & (c < base + 6), 1.0, 0.0)
    return kmat.astype(BF16), qmat.astype(BF16)


def _attn_kernel(*refs, kind, sq, skp, off, tq, tk, has_cache):
    it = iter(refs)
    q_ref, k_ref, v_ref = next(it), next(it), next(it)
    f_ref = next(it) if kind == "fox" else None
    ck_ref, cv_ref = (next(it), next(it)) if has_cache else (None, None)
    o_ref = next(it)
    kbuf, vbuf = (next(it), next(it)) if has_cache else (None, None)
    ka = (next(it), next(it)) if kind == "fox" else None
    va = (next(it), next(it)) if kind != "sb" else None

    if has_cache:
        past = ck_ref.shape[1]

        def fill(i, _):
            r0 = pl.multiple_of(i * ROW_CHUNK, ROW_CHUNK)
            kbuf[pl.ds(r0, ROW_CHUNK), :] = ck_ref[0, pl.ds(r0, ROW_CHUNK), :].astype(BF16)
            vbuf[pl.ds(r0, ROW_CHUNK), :] = cv_ref[0, pl.ds(r0, ROW_CHUNK), :].astype(BF16)
            return 0

        lax.fori_loop(0, past // ROW_CHUNK, fill, 0)
        kbuf[past:past + sq, :] = k_ref[0]
        vbuf[past:past + sq, :] = v_ref[0]
        if skp > past + sq:
            kbuf[past + sq:, :] = jnp.zeros((skp - past - sq, kbuf.shape[1]), BF16)
            vbuf[past + sq:, :] = jnp.zeros((skp - past - sq, LANES), BF16)
        kget = lambda r0, n: kbuf[pl.ds(r0, n), :]
        vget = lambda r0, n: vbuf[pl.ds(r0, n), :]
    else:
        kget = lambda r0, n: k_ref[0, pl.ds(r0, n), :]
        vget = lambda r0, n: v_ref[0, pl.ds(r0, n), :]

    if kind == "fox":
        kmat, qmat = _forget_placement(pl.program_id(1))
    if kind != "sb":
        lo_lanes = _half_mask((skp, LANES))
        vc = vget(0, skp)
        va[0][...] = jnp.where(lo_lanes, vc, jnp.ones_like(vc))
        va[1][...] = jnp.where(lo_lanes, jnp.ones_like(vc), vc)
        if kind == "fox":
            kc = kget(0, skp)
            e = _dot(f_ref[0], kmat).astype(BF16)
            ka[0][...] = jnp.where(lo_lanes, kc, e)
            ka[1][...] = jnp.where(lo_lanes, e, kc)

    nq = sq // tq
    row = lax.broadcasted_iota(jnp.int32, (tq, tk), 0)
    col = lax.broadcasted_iota(jnp.int32, (tq, tk), 1)
    lane_q = lax.broadcasted_iota(jnp.int32, (tq, LANES), 1)
    lo_half = lane_q < HEAD_DIM
    if kind == "sb":
        kr = lax.broadcasted_iota(jnp.int32, (2 * tk, tk), 0)
        kr = jnp.where(kr >= tk, kr - tk, kr)
        kc_ = lax.broadcasted_iota(jnp.int32, (2 * tk, tk), 1)
        later = jnp.where(kr > kc_, 1.0, 0.0).astype(BF16)

    def qblock(qi, _):
        q0 = qi * tq if isinstance(qi, int) else pl.multiple_of(qi * tq, tq)
        qblk = q_ref[0, pl.ds(q0, tq), :]
        qs = []
        if kind == "fox":
            eq = _dot(f_ref[0, pl.ds(off + q0, tq), :], qmat).astype(BF16)
        for hh in range(2):
            own = lo_half if hh == 0 else ~lo_half
            if kind == "mla":
                qs.append(qblk[:, hh * LANES:(hh + 1) * LANES])
            elif kind == "sb":
                qs.append(jnp.where(own, qblk, jnp.zeros_like(qblk)))
            else:
                qs.append(jnp.where(own, qblk, eq))

        def kload(hh, k0):
            if kind == "fox":
                return ka[hh][pl.ds(k0, tk), :]
            if kind == "mla":
                return kget(k0, tk)[:, hh * LANES:(hh + 1) * LANES]
            return kget(k0, tk)

        if kind == "sb":
            assert nq == 1
            nblk = skp // tk
            rowf = lax.broadcasted_iota(jnp.int32, (tq, skp), 0)
            colf = lax.broadcasted_iota(jnp.int32, (tq, skp), 1)
            vis = colf < (off + rowf)
            vall = vget(0, skp)
            outs = []
            for hh in range(2):
                z = _dot_nt(qs[hh], kget(0, skp))
                nl = jnp.where(vis, _softplus(z), 0.0)
                hi = nl.astype(BF16)
                lo = (nl - hi.astype(F32)).astype(BF16)
                within = [_dot(jnp.concatenate([hi[:, j * tk:(j + 1) * tk], lo[:, j * tk:(j + 1) * tk]],
                                               axis=1), later) for j in range(nblk)]
                after = jnp.zeros((tq, 1), F32)
                parts = [None] * nblk
                for j in reversed(range(nblk)):
                    parts[j] = within[j] + after
                    after = after + within[j][:, 0:1] + nl[:, j * tk:j * tk + 1]
                suf = jnp.concatenate(parts, axis=1)
                a = jnp.where(vis, jnp.exp(z - (nl + suf)), 0.0)
                outs.append(_dot(a.astype(BF16), vall))
        else:
            if kind == "fox":
                n_full = (off + q0 + 1) // tk
                n_total = (off + q0 + tq + tk - 1) // tk
            else:
                n_full = ((off + q0) // CHUNK * CHUNK + CHUNK) // tk
                n_total = (((off + q0 + tq - 1) // CHUNK + 1) * CHUNK + tk - 1) // tk

            def kstep(j, carry, masked):
                k0 = j * tk if isinstance(j, int) else pl.multiple_of(j * tk, tk)
                new = []
                for hh in range(2):
                    m, acc = carry[hh]
                    s = _dot_nt(qs[hh], kload(hh, k0))
                    if masked:
                        qpos = off + q0 + row
                        kpos = k0 + col
                        vis = (kpos <= qpos) if kind == "fox" else ((kpos // CHUNK) <= (qpos // CHUNK))
                        s = jnp.where(vis, s, NEG)
                    m_new = jnp.maximum(m, jnp.max(s, axis=-1, keepdims=True))
                    p = jnp.exp(s - m_new)
                    pv = _dot(p.astype(BF16), va[hh][pl.ds(k0, tk), :])
                    new.append((m_new, jnp.exp(m - m_new) * acc + pv))
                return tuple(new)

            init = ((jnp.full((tq, 1), NEG, F32), jnp.zeros((tq, LANES), F32)),) * 2
            carry = _loop(0, n_full, functools.partial(kstep, masked=False), init)
            carry = _loop(n_full, n_total, functools.partial(kstep, masked=True), carry)
            outs = [c[1] / pltpu.roll(c[1], HEAD_DIM, 1) for c in carry]
        o_ref[0, pl.ds(q0, tq), :] = jnp.where(lo_half, outs[0], outs[1]).astype(o_ref.dtype)
        return 0

    _loop(0, nq, qblock, 0) if nq == 1 else lax.fori_loop(0, nq, qblock, 0)


def _attention(kind, q, k, v, off, tq, tk, skp, f=None, cache=None):
    b, sq, qw = q.shape
    sn = k.shape[1]
    ql = qw // N_PAIRS
    pair = lambda n, w: pl.BlockSpec((1, n, w), lambda i, p: (i, 0, p))
    in_specs = [pair(sq, ql), pair(sn, ql), pair(sn, LANES)]
    args = [q, k, v]
    scratch = []
    if kind == "fox":
        in_specs.append(pl.BlockSpec((1, skp, LANES), lambda i, p: (i, 0, 0)))
        args.append(f)
    if cache is not None:
        ck, cv = cache
        in_specs += [pair(ck.shape[1], ql), pair(cv.shape[1], LANES)]
        args += [ck, cv]
        scratch += [pltpu.VMEM((skp, ql), BF16), pltpu.VMEM((skp, LANES), BF16)]
    else:
        assert skp == sn
    if kind == "fox":
        scratch += [pltpu.VMEM((skp, LANES), BF16)] * 2
    if kind != "sb":
        scratch += [pltpu.VMEM((skp, LANES), BF16)] * 2
    return pl.pallas_call(
        functools.partial(_attn_kernel, kind=kind, sq=sq, skp=skp, off=off, tq=tq, tk=tk,
                          has_cache=cache is not None),
        grid=(b, N_PAIRS),
        in_specs=in_specs,
        out_specs=pair(sq, LANES),
        out_shape=jax.ShapeDtypeStruct((b, sq, N_HEADS * HEAD_DIM), BF16),
        scratch_shapes=scratch,
        compiler_params=_cparams("parallel", "parallel"),
        name="attn_" + kind,
    )(*args)


def _attn_t_kernel(*refs, kind, sq, off, tq, tk):
    it = iter(refs)
    q_ref, k_ref, v_ref = next(it), next(it), next(it)
    f_ref = next(it) if kind == "fox" else None
    o_ref = next(it)
    ka = (next(it), next(it)) if kind == "fox" else None
    vt = (next(it), next(it)) if kind != "sb" else (next(it),)
    skp = k_ref.shape[1]

    lane_c = lax.broadcasted_iota(jnp.int32, (tk, LANES), 1)
    chan = lax.broadcasted_iota(jnp.int32, (LANES, tk), 0)
    if kind == "fox":
        kmat, qmat = _forget_placement(pl.program_id(1))

    def build(i, _):
        r0 = pl.multiple_of(i * tk, tk)
        vct = v_ref[0, pl.ds(r0, tk), :].astype(F32).T
        if kind == "sb":
            vt[0][:, pl.ds(r0, tk)] = vct.astype(BF16)
        else:
            vt[0][:, pl.ds(r0, tk)] = jnp.where(chan < HEAD_DIM, vct, 1.0).astype(BF16)
            vt[1][:, pl.ds(r0, tk)] = jnp.where(chan >= HEAD_DIM, vct, 1.0).astype(BF16)
        if kind == "fox":
            kc = k_ref[0, pl.ds(r0, tk), :]
            e = _dot(f_ref[0, pl.ds(r0, tk), :], kmat).astype(BF16)
            ka[0][pl.ds(r0, tk), :] = jnp.where(lane_c < HEAD_DIM, kc, e)
            ka[1][pl.ds(r0, tk), :] = jnp.where(lane_c >= HEAD_DIM, kc, e)
        return 0

    lax.fori_loop(0, skp // tk, build, 0)

    nq = sq // tq
    krow = lax.broadcasted_iota(jnp.int32, (tk, tq), 0)
    qcol = lax.broadcasted_iota(jnp.int32, (tk, tq), 1)
    lane_q = lax.broadcasted_iota(jnp.int32, (tq, LANES), 1)
    lo_half = lane_q < HEAD_DIM
    out_lo = lax.broadcasted_iota(jnp.int32, (LANES, tq), 0) < HEAD_DIM
    if kind == "sb":
        a_ = lax.broadcasted_iota(jnp.int32, (LANES, 2 * LANES), 0)
        b_ = lax.broadcasted_iota(jnp.int32, (LANES, 2 * LANES), 1)
        b_ = jnp.where(b_ >= LANES, b_ - LANES, b_)
        later_t = jnp.where(b_ > a_, 1.0, 0.0).astype(BF16)

    def qblock(qi):
        q0 = qi * tq
        qblk = q_ref[0, q0:q0 + tq, :]
        qs = []
        if kind == "fox":
            eq = _dot(f_ref[0, off + q0:off + q0 + tq, :], qmat).astype(BF16)
        for hh in range(2):
            own = lo_half if hh == 0 else ~lo_half
            if kind == "mla":
                qs.append(qblk[:, hh * LANES:(hh + 1) * LANES])
            elif kind == "sb":
                qs.append(jnp.where(own, qblk, jnp.zeros_like(qblk)))
            else:
                qs.append(jnp.where(own, qblk, eq))

        def kload(hh, k0):
            if kind == "fox":
                return ka[hh][k0:k0 + tk, :]
            if kind == "mla":
                return k_ref[0, k0:k0 + tk, hh * LANES:(hh + 1) * LANES]
            return k_ref[0, k0:k0 + tk, :]

        if kind == "sb":
            n_full = (off + q0) // tk
            n_total = (off + q0 + tq - 1 + tk - 1) // tk

            def stage1(j, masked):
                k0 = j * tk
                res = []
                for hh in range(2):
                    z = _dot_nt(kload(hh, k0), qs[hh])
                    nl = _softplus(z)
                    if masked:
                        vis = (k0 + krow) < (off + q0 + qcol)
                        nl = jnp.where(vis, nl, 0.0)
                        z = jnp.where(vis, z, NEG)
                    hi = nl.astype(BF16)
                    res.append((z, nl, hi, (nl - hi.astype(F32)).astype(BF16)))
                return res

            def stage2(j, state, st1):
                vb = vt[0][:, j * tk:(j + 1) * tk]
                new = []
                for hh in range(2):
                    rem, acc = state[hh]
                    z, nl, hi, lo = st1[hh]
                    after = rem
                    parts = []
                    for sb in reversed(range(tk // LANES)):
                        rs = slice(sb * LANES, (sb + 1) * LANES)
                        within = _dot(later_t, jnp.concatenate([hi[rs], lo[rs]], axis=0))
                        parts.append(within + after)
                        after = after + within[0:1, :] + nl[sb * LANES:sb * LANES + 1, :]
                    suf = jnp.concatenate(parts[::-1], axis=0)
                    a = jnp.exp(z - (nl + suf))
                    new.append((after, acc + _dot(vb, a.astype(BF16))))
                return new

            state = [(jnp.zeros((1, tq), F32), jnp.zeros((LANES, tq), F32))] * 2
            st1 = stage1(n_full, True)
            for j in range(n_full, -1, -1):
                nxt = stage1(j - 1, False) if j > 0 else None
                state = stage2(j, state, st1)
                st1 = nxt
            carry = state
            outs = [c[1] for c in carry]
        else:
            if kind == "fox":
                n_full = (off + q0 + 1) // tk
                n_total = (off + q0 + tq + tk - 1) // tk
            else:
                n_full = ((off + q0) // CHUNK * CHUNK + CHUNK) // tk
                n_total = (((off + q0 + tq - 1) // CHUNK + 1) * CHUNK + tk - 1) // tk

            def scores(j):
                return [_dot_nt(kload(hh, j * tk), qs[hh]) for hh in range(2)]

            s_cur = scores(0)
            m = [jnp.full((1, tq), NEG, F32)] * 2
            acc, p_prev, a_prev = [None, None], [None, None], [None, None]
            for j in range(n_total):
                s_next = scores(j + 1) if j + 1 < n_total else None
                for hh in range(2):
                    if j > 0:
                        pv = _dot(vt[hh][:, (j - 1) * tk:j * tk], p_prev[hh])
                        acc[hh] = pv if j == 1 else a_prev[hh] * acc[hh] + pv
                    s = s_cur[hh]
                    if j >= n_full:
                        qpos = off + q0 + qcol
                        kpos = j * tk + krow
                        vis = (kpos <= qpos) if kind == "fox" else ((kpos // CHUNK) <= (qpos // CHUNK))
                        s = jnp.where(vis, s, NEG)
                    m_new = jnp.maximum(m[hh], jnp.max(s, axis=0, keepdims=True))
                    p_prev[hh] = jnp.exp(s - m_new).astype(BF16)
                    a_prev[hh] = jnp.exp(m[hh] - m_new)
                    m[hh] = m_new
                s_cur = s_next
            accs = []
            for hh in range(2):
                pv = _dot(vt[hh][:, (n_total - 1) * tk:n_total * tk], p_prev[hh])
                accs.append(pv if n_total == 1 else a_prev[hh] * acc[hh] + pv)
            outs = [accs[0] / accs[0][HEAD_DIM:HEAD_DIM + 1, :], accs[1] / accs[1][0:1, :]]
        o_ref[0, q0:q0 + tq, :] = jnp.where(out_lo, outs[0], outs[1]).T.astype(o_ref.dtype)

    for qi in range(nq):
        qblock(qi)


def _attention_t(kind, q, k, v, off, tq, tk, f=None):
    b, sq, qw = q.shape
    skp = k.shape[1]
    ql = qw // N_PAIRS
    pair = lambda n, w: pl.BlockSpec((1, n, w), lambda i, p: (i, 0, p))
    in_specs = [pair(sq, ql), pair(skp, ql), pair(skp, LANES)]
    args = [q, k, v]
    scratch = []
    if kind == "fox":
        in_specs.append(pl.BlockSpec((1, skp, LANES), lambda i, p: (i, 0, 0)))
        args.append(f)
        scratch += [pltpu.VMEM((skp, LANES), BF16)] * 2
    scratch += [pltpu.VMEM((LANES, skp), BF16)] * (1 if kind == "sb" else 2)
    return pl.pallas_call(
        functools.partial(_attn_t_kernel, kind=kind, sq=sq, off=off, tq=tq, tk=tk),
        grid=(b, N_PAIRS),
        in_specs=in_specs,
        out_specs=pair(sq, LANES),
        out_shape=jax.ShapeDtypeStruct((b, sq, N_HEADS * HEAD_DIM), BF16),
        scratch_shapes=scratch,
        compiler_params=_cparams("parallel", "parallel"),
        name="attn_t_" + kind,
    )(*args)


SWA_TQ = 2 * CHUNK


def _swa_span(tq):
    return _round_up(tq + WINDOW, LANES)


def _swa_kernel(slope_ref, sink_ref, q_ref, k_ref, v_ref, o_ref, vta, vtb, *, sq, past, tq):
    nq = sq // tq
    kv = pl.program_id(1)
    sk = k_ref.shape[1]
    chan = lax.broadcasted_iota(jnp.int32, (LANES, ROW_CHUNK), 0)

    def build(i, _):
        r0 = pl.multiple_of(i * ROW_CHUNK, ROW_CHUNK)
        vct = v_ref[0, pl.ds(r0, ROW_CHUNK), :].astype(F32).T
        vta[:, pl.ds(r0, ROW_CHUNK)] = jnp.where(chan < HEAD_DIM, vct, 1.0).astype(BF16)
        vtb[:, pl.ds(r0, ROW_CHUNK)] = jnp.where(chan >= HEAD_DIM, vct, 1.0).astype(BF16)
        return 0

    lax.fori_loop(0, sk // ROW_CHUNK, build, 0)

    span = _swa_span(tq)
    krow = lax.broadcasted_iota(jnp.int32, (span, tq), 0)
    qcol = lax.broadcasted_iota(jnp.int32, (span, tq), 1)
    lo_half = _half_mask((tq, LANES))
    out_lo = lax.broadcasted_iota(jnp.int32, (LANES, tq), 0) < HEAD_DIM

    def qblock(qi, _):
        q0 = qi * tq if isinstance(qi, int) else pl.multiple_of(qi * tq, tq)
        ks = jnp.maximum(q0 + past - WINDOW, 0)
        ks = ks if isinstance(ks, int) else pl.multiple_of(ks, LANES)
        kwin = k_ref[0, pl.ds(ks, span), :]
        qpos = q0 + qcol
        kpos = ks - past + krow
        qc = qpos // CHUNK
        vis = (kpos >= qc * CHUNK - WINDOW) & (kpos < (qc + 1) * CHUNK)
        dist = jnp.abs(qpos - kpos).astype(F32)
        for pair in range(SWA_GROUPS // 2):
            qpair = q_ref[0, pl.ds(q0, tq), pair * LANES:(pair + 1) * LANES]
            outs = []
            for half in range(2):
                head = kv * SWA_GROUPS + pair * 2 + half
                qh = jnp.where(lo_half if half == 0 else ~lo_half, qpair, jnp.zeros_like(qpair))
                s = _dot_nt(kwin, qh) - slope_ref[head] * dist
                s = jnp.where(vis, s, NEG)
                sink = sink_ref[head]
                m = jnp.maximum(jnp.max(s, axis=0, keepdims=True), sink)
                e = jnp.exp(s - m).astype(BF16)
                vt = vta if half == 0 else vtb
                ot = _dot(vt[:, pl.ds(ks, span)], e)
                den = (ot[HEAD_DIM:HEAD_DIM + 1, :] if half == 0 else ot[0:1, :]) + jnp.exp(sink - m)
                outs.append(ot / den)
            o_ref[0, pl.ds(q0, tq), pair * LANES:(pair + 1) * LANES] = (
                jnp.where(out_lo, outs[0], outs[1]).T.astype(o_ref.dtype))
        return 0

    _loop(0, nq, qblock, 0) if nq == 1 else lax.fori_loop(0, nq, qblock, 0)


def _swa_attn(q, kdup, vdup, slopes, sinks, past, tq):
    b, sq, _ = q.shape
    sk = kdup.shape[1]
    gw = SWA_GROUPS * HEAD_DIM
    smem = pl.BlockSpec(memory_space=pltpu.SMEM)
    return pl.pallas_call(
        functools.partial(_swa_kernel, sq=sq, past=past, tq=tq),
        grid=(b, SWA_KV_HEADS),
        in_specs=[smem, smem,
                  pl.BlockSpec((1, sq, gw), lambda i, p: (i, 0, p)),
                  pl.BlockSpec((1, sk, LANES), lambda i, p: (i, 0, p)),
                  pl.BlockSpec((1, sk, LANES), lambda i, p: (i, 0, p))],
        out_specs=pl.BlockSpec((1, sq, gw), lambda i, p: (i, 0, p)),
        out_shape=jax.ShapeDtypeStruct((b, sq, N_HEADS * HEAD_DIM), BF16),
        scratch_shapes=[pltpu.VMEM((LANES, sk), BF16)] * 2,
        compiler_params=_cparams("parallel", "parallel"),
        name="attn_swa",
    )(slopes, sinks, q, kdup, vdup)


MLA_IN_COLS = MLA_Q_RANK + MLA_KV_RANK + 2 * LANES
MLA_HEAD_W = LANES


def _rms_norm(x, g):
    return x * lax.rsqrt(jnp.mean(x * x, axis=-1, keepdims=True) + 1e-6) * g


def _mla_proj_kernel(x_ref, sc_ref, sh_ref, win_ref, gq_ref, gkv_ref, wqa_ref, wqb_ref,
                     wk_ref, wv_ref, qc_ref, qs_ref, kc_ref, ks_ref,
                     q_ref, kf_ref, v_ref, ckv_ref, kr_ref):
    h = (x_ref[0] * (1.0 + sc_ref[0]) + sh_ref[0]).astype(BF16)
    proj = _dot(h, win_ref[...])
    cq = _rms_norm(proj[:, :MLA_Q_RANK], gq_ref[...]).astype(BF16)
    ckv = _rms_norm(proj[:, MLA_Q_RANK:MLA_Q_RANK + MLA_KV_RANK], gkv_ref[...])
    ckv_ref[0] = ckv
    ckv_b = ckv.astype(BF16)
    kr0 = MLA_Q_RANK + MLA_KV_RANK
    krr = proj[:, kr0:kr0 + LANES] * kc_ref[...] + proj[:, kr0 + LANES:kr0 + 2 * LANES] * ks_ref[...]
    kr_ref[0] = pltpu.roll(krr, LANES - MLA_D_NOPE, 1)[:, :MLA_D_ROPE]
    qa = _dot(cq, wqa_ref[...])
    qb = _dot(cq, wqb_ref[...])
    kn = _dot(ckv_b, wk_ref[...])
    v_ref[0] = _dot(ckv_b, wv_ref[...]).astype(BF16)
    qc, qs = qc_ref[...], qs_ref[...]
    for hd in range(N_HEADS):
        sl = slice(hd * MLA_HEAD_W, (hd + 1) * MLA_HEAD_W)
        q_ref[0, :, sl] = (qa[:, sl] * qc + qb[:, sl] * qs).astype(BF16)
        kf_ref[0, :, sl] = (kn[:, sl] + krr).astype(BF16)


def _mla_proj(x, sc, sh, mw, tables, tm=256):
    b, s, d = x.shape
    tm = _row_tile(s, tm)
    qc, qs, kc, ks = tables
    full = lambda a: pl.BlockSpec(a.shape, lambda i, j: (0,) * a.ndim)
    tab = pl.BlockSpec((tm, LANES), lambda i, j: (j, 0))
    row = lambda n: pl.BlockSpec((1, tm, n), lambda i, j: (i, j, 0))
    hw = N_HEADS * MLA_HEAD_W
    return pl.pallas_call(
        _mla_proj_kernel,
        grid=(b, s // tm),
        in_specs=[row(d),
                  pl.BlockSpec((1, 1, d), lambda i, j: (i, 0, 0)),
                  pl.BlockSpec((1, 1, d), lambda i, j: (i, 0, 0)),
                  full(mw["win"]), full(mw["gq"]), full(mw["gkv"]), full(mw["wqa"]),
                  full(mw["wqb"]), full(mw["wk"]), full(mw["wv"]), tab, tab, tab, tab],
        out_specs=[row(hw), row(hw), row(N_HEADS * HEAD_DIM), row(MLA_KV_RANK), row(MLA_D_ROPE)],
        out_shape=[jax.ShapeDtypeStruct((b, s, hw), BF16),
                   jax.ShapeDtypeStruct((b, s, hw), BF16),
                   jax.ShapeDtypeStruct((b, s, N_HEADS * HEAD_DIM), BF16),
                   jax.ShapeDtypeStruct((b, s, MLA_KV_RANK), F32),
                   jax.ShapeDtypeStruct((b, s, MLA_D_ROPE), F32)],
        compiler_params=_cparams("parallel", "parallel"),
        name="proj_mla",
    )(x, sc, sh, mw["win"], mw["gq"], mw["gkv"], mw["wqa"], mw["wqb"], mw["wk"], mw["wv"],
      qc, qs, kc, ks)


def _mla_cache_kernel(ckv_ref, kr_ref, wk_ref, wv_ref, place_ref, kf_ref, v_ref):
    ckv_b = ckv_ref[0].astype(BF16)
    kf_ref[0] = (_dot(ckv_b, wk_ref[...]) + _dot(kr_ref[0], place_ref[...])).astype(BF16)
    v_ref[0] = _dot(ckv_b, wv_ref[...]).astype(BF16)


def _mla_cache_up(ckv, kr_pad, mw, tm=512):
    b, p, r = ckv.shape
    tm = _row_tile(p, tm)
    hw = N_HEADS * MLA_HEAD_W
    full = lambda a: pl.BlockSpec(a.shape, lambda i, j: (0,) * a.ndim)
    row = lambda n: pl.BlockSpec((1, tm, n), lambda i, j: (i, j, 0))
    return pl.pallas_call(
        _mla_cache_kernel,
        grid=(b, p // tm),
        in_specs=[row(r), row(LANES), full(mw["wk"]), full(mw["wv"]), full(mw["place"])],
        out_specs=[row(hw), row(N_HEADS * HEAD_DIM)],
        out_shape=[jax.ShapeDtypeStruct((b, p, hw), BF16),
                   jax.ShapeDtypeStruct((b, p, N_HEADS * HEAD_DIM), BF16)],
        compiler_params=_cparams("parallel", "parallel"),
        name="mla_cache_up",
    )(ckv, kr_pad, mw["wk"], mw["wv"], mw["place"])


def _xor_partner(x, k, idx, axis):
    n = x.shape[axis]
    up = pltpu.roll(x, n - k, axis)
    dn = pltpu.roll(x, k, axis)
    return jnp.where((idx & k) == 0, up, dn)


def _argmax_groups(v, idx, pos, strides, axis):
    for k in strides:
        pv = _xor_partner(v, k, pos, axis)
        pi = _xor_partner(idx, k, pos, axis)
        take = (pv > v) | ((pv == v) & (pi < idx))
        v = jnp.where(take, pv, v)
        idx = jnp.where(take, pi, idx)
    return v, idx


def _route(logits, rb):
    ex = lax.broadcasted_iota(jnp.int32, logits.shape, 0)
    scores = _sigmoid(logits)
    biased = scores + rb
    p1 = _xor_partner(biased, 1, ex, 0)
    hi1, lo1 = jnp.maximum(biased, p1), jnp.minimum(biased, p1)
    hi2, lo2 = _xor_partner(hi1, 2, ex, 0), _xor_partner(lo1, 2, ex, 0)
    group_score = jnp.maximum(hi1, hi2) + jnp.maximum(jnp.minimum(hi1, hi2), jnp.maximum(lo1, lo2))
    gid = ex >> 2
    _, best = _argmax_groups(group_score, gid, ex, (4, 8), 0)
    cand = jnp.where(gid == best, biased, NEG)
    _, i1 = _argmax_groups(cand, ex, ex, (1, 2, 4, 8), 0)
    _, i2 = _argmax_groups(jnp.where(ex == i1, -jnp.inf, cand), ex, ex, (1, 2, 4, 8), 0)
    sel = jnp.where((ex == i1) | (ex == i2), scores, 0.0)
    return sel / jnp.sum(sel, axis=0, keepdims=True)


def _outproj_kernel(o_ref, x_ref, w_ref, ga_ref, scf_ref, shf_ref, g_ref, b_ref,
                    rwh_ref, rwl_ref, rb_ref, x1_ref, h2_ref, gate_ref):
    y = DEEPNORM_ALPHA * x_ref[0] + (1.0 + ga_ref[0]) * _dot(o_ref[0], w_ref[...])
    x1 = _layer_norm(y, g_ref[...], b_ref[...])
    x1_ref[0] = x1
    h2 = x1 * (1.0 + scf_ref[0]) + shf_ref[0]
    hi = h2.astype(BF16)
    h2_ref[0] = hi
    lo = (h2 - hi.astype(F32)).astype(BF16)
    logits = _dot_nt(rwh_ref[...], hi) + _dot_nt(rwh_ref[...], lo) + _dot_nt(rwl_ref[...], hi)
    gates = _route(logits, rb_ref[...])
    pad = jnp.zeros((LANES - N_EXPERTS, gates.shape[1]), F32)
    gate_ref[0] = jnp.concatenate([gates, pad], axis=0).T[:, :N_EXPERTS]


def _outproj(o, x, w, ga, scf, shf, g, bta, rwh, rwl, rb, tm=512):
    b, s, d = x.shape
    tm = _row_tile(s, tm)
    row = lambda n: pl.BlockSpec((1, tm, n), lambda i, j: (i, j, 0))
    mod = pl.BlockSpec((1, 1, d), lambda i, j: (i, 0, 0))
    full = lambda a: pl.BlockSpec(a.shape, lambda i, j: (0,) * a.ndim)
    return pl.pallas_call(
        _outproj_kernel,
        grid=(b, s // tm),
        in_specs=[row(d), row(d), full(w), mod, mod, mod, full(g), full(bta),
                  full(rwh), full(rwl), full(rb)],
        out_specs=[row(d), row(d), row(N_EXPERTS)],
        out_shape=[jax.ShapeDtypeStruct((b, s, d), F32), jax.ShapeDtypeStruct((b, s, d), BF16),
                   jax.ShapeDtypeStruct((b, s, N_EXPERTS), F32)],
        compiler_params=_cparams("parallel", "parallel"),
        name="outproj_ln_router",
    )(o, x, w, ga, scf, shf, g, bta, rwh, rwl, rb)


EXPERT_UNROLL = 4


def _moe_kernel(h_ref, gate_ref, x_ref, gf_ref, g_ref, b_ref, wg_ref, wu_ref, wd_ref,
                o_ref, acc_ref):
    bt, tm, d = h_ref.shape
    rows = bt * tm
    ne, _, de = wg_ref.shape
    h = h_ref[...].reshape(rows, d)
    gates = gate_ref[...].reshape(rows, N_EXPERTS)
    lane = lax.broadcasted_iota(jnp.int32, gates.shape, 1)
    acc_ref[...] = jnp.zeros_like(acc_ref)

    def body(i, _):
        e0 = i * EXPERT_UNROLL
        acts = []
        for u in range(EXPERT_UNROLL):
            e = e0 + u
            hg = _dot(h, wg_ref[e])
            hu = _dot(h, wu_ref[e])
            ge = jnp.sum(jnp.where(lane == e, gates, 0.0), axis=-1, keepdims=True)
            acts.append((hg * _sigmoid(hg) * hu * ge).astype(BF16))
        wd = wd_ref[pl.ds(e0, EXPERT_UNROLL)].reshape(EXPERT_UNROLL * de, d)
        acc_ref[...] += _dot(jnp.concatenate(acts, axis=1), wd)
        return 0

    lax.fori_loop(0, ne // EXPERT_UNROLL, body, 0)
    y = DEEPNORM_ALPHA * x_ref[...] + (1.0 + gf_ref[...]) * acc_ref[...].reshape(bt, tm, d)
    o_ref[...] = _layer_norm(y, g_ref[...], b_ref[...])


def _moe(h2, gates, x1, gf, g, bta, wg, wu, wd, rows=512):
    b, s, d = x1.shape
    tm = _row_tile(s, rows)
    bt = min(b, rows // tm)
    row = lambda n: pl.BlockSpec((bt, tm, n), lambda i, j: (i, j, 0))
    full = lambda a: pl.BlockSpec(a.shape, lambda i, j: (0,) * a.ndim)
    once = lambda a: pl.BlockSpec(a.shape, lambda i, j: (0,) * a.ndim, pipeline_mode=pl.Buffered(1))
    return pl.pallas_call(
        _moe_kernel,
        grid=(b // bt, s // tm),
        in_specs=[row(d), row(N_EXPERTS), row(d),
                  pl.BlockSpec((bt, 1, d), lambda i, j: (i, 0, 0)), full(g), full(bta),
                  once(wg), once(wu), once(wd)],
        out_specs=row(d),
        out_shape=jax.ShapeDtypeStruct((b, s, d), F32),
        scratch_shapes=[pltpu.VMEM((bt * tm, d), F32)],
        compiler_params=_cparams("parallel", "parallel"),
        name="moe_ffn",
    )(h2, gates, x1, gf, g, bta, wg, wu, wd)


def _pad_cols(w, n):
    return jnp.pad(w, ((0, 0), (0, n - w.shape[1])))


def _rot_half_cols(w):
    half = w.shape[-1] // 2
    return jnp.concatenate([-w[..., half:], w[..., :half]], axis=-1)


def _mla_weights(mla_w_in, mla_q_norm, mla_w_uq, mla_kv_norm, mla_w_uk, mla_w_uv):
    d = mla_w_in.shape[0]
    kr0 = MLA_Q_RANK + MLA_KV_RANK
    wkr = mla_w_in[:, kr0:]
    z = lambda n: jnp.zeros((d, n), F32)
    win = jnp.concatenate([mla_w_in[:, :kr0], z(MLA_D_NOPE), wkr, z(LANES - MLA_D_NOPE - MLA_D_ROPE),
                           z(MLA_D_NOPE), _rot_half_cols(wkr), z(LANES - MLA_D_NOPE - MLA_D_ROPE)],
                          axis=1)
    wuq = mla_w_uq.reshape(MLA_Q_RANK, N_HEADS, MLA_D_NOPE + MLA_D_ROPE)
    zq = lambda n: jnp.zeros((MLA_Q_RANK, N_HEADS, n), F32)
    tail = LANES - MLA_D_NOPE - MLA_D_ROPE
    wqa = jnp.concatenate([wuq, zq(tail)], axis=-1).reshape(MLA_Q_RANK, N_HEADS * LANES)
    wqb = jnp.concatenate([zq(MLA_D_NOPE), _rot_half_cols(wuq[..., MLA_D_NOPE:]), zq(tail)],
                          axis=-1).reshape(MLA_Q_RANK, N_HEADS * LANES)
    wk = jnp.concatenate([mla_w_uk, jnp.zeros((MLA_KV_RANK, N_HEADS, LANES - MLA_D_NOPE), F32)],
                         axis=-1).reshape(MLA_KV_RANK, N_HEADS * LANES)
    wv = mla_w_uv.reshape(MLA_KV_RANK, N_HEADS * HEAD_DIM)
    src = jnp.arange(LANES)[:, None]
    dst = jnp.arange(N_HEADS * LANES)[None, :]
    place = ((dst % LANES) == src + MLA_D_NOPE) & (src < MLA_D_ROPE)
    return dict(win=win.astype(BF16), gq=mla_q_norm.reshape(1, -1), gkv=mla_kv_norm.reshape(1, -1),
                wqa=wqa.astype(BF16), wqb=wqb.astype(BF16), wk=wk.astype(BF16),
                wv=wv.astype(BF16), place=place.astype(BF16))


def _rope_tables(n_pos):
    half = MLA_D_ROPE // 2
    inv = ROPE_THETA ** (-jnp.arange(half, dtype=F32) * 2.0 / MLA_D_ROPE)
    ang = jnp.arange(n_pos, dtype=F32)[:, None] * inv[None, :]
    cos = jnp.concatenate([jnp.cos(ang), jnp.cos(ang)], axis=-1)
    sin = jnp.concatenate([jnp.sin(ang), jnp.sin(ang)], axis=-1)
    scale = (MLA_D_NOPE + MLA_D_ROPE) ** -0.5
    z = lambda n: jnp.zeros((n_pos, n), F32)
    tail = LANES - MLA_D_NOPE - MLA_D_ROPE
    qc = jnp.concatenate([jnp.full((n_pos, MLA_D_NOPE), scale, F32), cos * scale, z(tail)], axis=1)
    qs = jnp.concatenate([z(MLA_D_NOPE), sin * scale, z(tail)], axis=1)
    kc = jnp.concatenate([z(MLA_D_NOPE), cos, z(tail)], axis=1)
    ks = jnp.concatenate([z(MLA_D_NOPE), sin, z(tail)], axis=1)
    return qc, qs, kc, ks


def _dup_heads(w):
    d = w.shape[0]
    w4 = w.reshape(d, SWA_KV_HEADS, HEAD_DIM)
    return jnp.concatenate([w4, w4], axis=-1).reshape(d, SWA_KV_HEADS * LANES)


def _pad_rows(a, n):
    return jnp.pad(a, ((0, 0), (0, n - a.shape[1]), (0, 0)))


def _round_up(n, m):
    return -(-n // m) * m


def _attn_tiles(kind, sq, n_keys):
    if sq % LONG_TQ == 0:
        return LONG_TQ, LONG_TQ, n_keys
    if kind == "sb":
        return sq, 256, _round_up(n_keys, 256)
    skp = _round_up(n_keys, LANES)
    return sq, skp, skp


def _flat_heads(a):
    return a.reshape(a.shape[0], a.shape[1], -1)


def _fox_mixer(x, sc, sh, fw, cache):
    b, s, _ = x.shape
    q, k32, kb, v32, vb, logf = _proj(x, sc, sh, fw["w"], [[BF16], [F32, BF16], [F32, BF16], ["logf"]],
                                      bias=fw["bias"], name="proj_fox")
    off = 0 if cache is None else cache[0].shape[1]
    tq, tk, skp = _attn_tiles("fox", s, off + s)
    lall = logf if cache is None else jnp.concatenate([cache[2], logf], axis=1)
    f = _forget_cumsum(_pad_rows(lall, skp))
    kv_cache = None if cache is None else (_flat_heads(cache[0]), _flat_heads(cache[1]))
    if cache is None:
        o = _attention_t("fox", q, kb, vb, off, tq, tk, f=f)
    else:
        o = _attention("fox", q, kb, vb, off, tq, tk, skp, f=f, cache=kv_cache)
    hshape = (b, s, N_HEADS, HEAD_DIM)
    return o, (k32.reshape(hshape), v32.reshape(hshape), logf)


def _mla_mixer(x, sc, sh, mw, cache):
    b, s, _ = x.shape
    off = 0 if cache is None else cache[0].shape[1]
    tables = [t[off:off + s] for t in mw["tables"]]
    q, kf, v, ckv, kr = _mla_proj(x, sc, sh, mw, tables)
    tq, tk, skp = _attn_tiles("mla", s, off + s)
    kv_cache = None
    if cache is not None:
        c_ckv, c_kr = cache
        kr_pad = jnp.pad(c_kr, ((0, 0), (0, 0), (0, LANES - MLA_D_ROPE))).astype(BF16)
        kv_cache = _mla_cache_up(c_ckv, kr_pad, mw)
    if cache is None:
        o = _attention_t("mla", q, kf, v, off, tq, tk)
    else:
        o = _attention("mla", q, kf, v, off, tq, tk, skp, cache=kv_cache)
    return o, (ckv, kr)


def _swa_mixer(x, sc, sh, sw, cache):
    b, s, _ = x.shape
    q, k32, kdup, v32, vdup = _proj(x, sc, sh, sw["w"], [[BF16], [F32], [BF16], [F32], [BF16]],
                                    name="proj_swa")
    kvshape = lambda a: a.reshape(a.shape[0], a.shape[1], SWA_KV_HEADS, HEAD_DIM)
    if cache is None:
        past, tq = 0, min(s, SWA_TQ)
        new_k, new_v = kvshape(k32[:, -WINDOW:]), kvshape(v32[:, -WINDOW:])
    else:
        ck, cv = cache
        past, tq = ck.shape[1], s
        dup = lambda c: jnp.concatenate([c, c], axis=-1).reshape(c.shape[0], c.shape[1], -1).astype(BF16)
        kdup = _pad_rows(jnp.concatenate([dup(ck), kdup], axis=1), _swa_span(tq))
        vdup = _pad_rows(jnp.concatenate([dup(cv), vdup], axis=1), _swa_span(tq))
        new_k = jnp.concatenate([ck, kvshape(k32)], axis=1)[:, -WINDOW:]
        new_v = jnp.concatenate([cv, kvshape(v32)], axis=1)[:, -WINDOW:]
    o = _swa_attn(q, kdup, vdup, sw["slopes"], sw["sinks"], past, tq)
    return o, (new_k, new_v)


def _sb_mixer(x, sc, sh, bw, cache):
    b, s, _ = x.shape
    q, k32, kb, v32, vb = _proj(x, sc, sh, bw["w"], [[BF16], [F32, BF16], [F32, BF16]], name="proj_sb")
    off = 0 if cache is None else cache[0].shape[1]
    tq, tk, skp = _attn_tiles("sb", s, off + s)
    kv_cache = None if cache is None else (_flat_heads(cache[0]), _flat_heads(cache[1]))
    if cache is None:
        o = _attention_t("sb", q, kb, vb, off, tq, tk)
    else:
        o = _attention("sb", q, kb, vb, off, tq, tk, skp, cache=kv_cache)
    hshape = (b, s, N_HEADS, HEAD_DIM)
    return o, (k32.reshape(hshape), v32.reshape(hshape))


_MIXERS = (_fox_mixer, _mla_mixer, _swa_mixer, _sb_mixer)


def _trunk(x, mods, caches, wts):
    states = []
    for i in range(DEPTH):
        m = [mods[i, :, j][:, None, :] for j in range(6)]
        sh_a, sc_a, g_a, sh_f, sc_f, g_f = m
        o, st = _MIXERS[i](x, sc_a, sh_a, wts["mixer"][i], caches[i])
        states.extend(st)
        x1, h2, gates = _outproj(o, x, wts["w_out"][i], g_a, sc_f, sh_f,
                                 wts["ln_g"][i, 0:1], wts["ln_b"][i, 0:1],
                                 wts["rw_hi"], wts["rw_lo"], wts["rb"])
        x = _moe(h2, gates, x1, g_f, wts["ln_g"][i, 1:2], wts["ln_b"][i, 1:2],
                 wts["moe_g"][i], wts["moe_u"][i], wts["moe_d"][i])
    return x, states


def kernel(x_prompt, x_sample, cache_fox_k, cache_fox_v, cache_fox_logf, cache_mla_ckv, cache_mla_krope, cache_swa_k, cache_swa_v, cache_sb_k, cache_sb_v, c_prompt, c_sample, ada_w, ada_b, ln_g, ln_b, fox_w_in, fox_b_f, fox_w_out, mla_w_in, mla_q_norm, mla_w_uq, mla_kv_norm, mla_w_uk, mla_w_uv, mla_w_out, swa_w_in, swa_sinks, swa_w_out, sb_w_in, sb_w_out, router_w, router_b, moe_w_gate, moe_w_up, moe_w_down):
    bp, sp, d = x_prompt.shape
    bs, ss, _ = x_sample.shape
    past = cache_fox_k.shape[1]
    hw = N_HEADS * HEAD_DIM
    qscale = HEAD_DIM ** -0.5

    fox = dict(w=[(fox_w_in[:, :hw] * qscale).astype(BF16), fox_w_in[:, hw:2 * hw].astype(BF16),
                  fox_w_in[:, 2 * hw:3 * hw].astype(BF16), _pad_cols(fox_w_in[:, 3 * hw:], LANES).astype(BF16)],
               bias=_pad_cols(fox_b_f.reshape(1, -1), LANES))
    mla = _mla_weights(mla_w_in, mla_q_norm, mla_w_uq, mla_kv_norm, mla_w_uk, mla_w_uv)
    mla["tables"] = _rope_tables(max(sp, past + ss))
    kvw = SWA_KV_HEADS * HEAD_DIM
    wk, wv = swa_w_in[:, hw:hw + kvw], swa_w_in[:, hw + kvw:]
    swa = dict(w=[(swa_w_in[:, :hw] * qscale).astype(BF16), wk.astype(BF16), _dup_heads(wk).astype(BF16),
                  wv.astype(BF16), _dup_heads(wv).astype(BF16)],
               slopes=jnp.exp2(-8.0 * jnp.arange(1, N_HEADS + 1, dtype=F32) / N_HEADS),
               sinks=swa_sinks.astype(F32))
    sb = dict(w=[(sb_w_in[:, :hw] * qscale).astype(BF16), sb_w_in[:, hw:2 * hw].astype(BF16),
                 sb_w_in[:, 2 * hw:].astype(BF16)])
    rw = router_w.T
    rw_hi = rw.astype(BF16)
    wts = dict(mixer=(fox, mla, swa, sb),
               w_out=[w.astype(BF16) for w in (fox_w_out, mla_w_out, swa_w_out, sb_w_out)],
               ln_g=ln_g, ln_b=ln_b, rw_hi=rw_hi, rw_lo=(rw - rw_hi.astype(F32)).astype(BF16),
               rb=router_b.reshape(-1, 1),
               moe_g=moe_w_gate.astype(BF16), moe_u=moe_w_up.astype(BF16), moe_d=moe_w_down.astype(BF16))

    mods = _ada_mod(jnp.concatenate([c_prompt, c_sample], axis=0), ada_w, ada_b)
    mods_p = mods[:, :bp].reshape(DEPTH, bp, 6, d)
    mods_s = mods[:, bp:].reshape(DEPTH, bs, 6, d)

    y_p, st_p = _trunk(x_prompt, mods_p, (None, None, None, None), wts)
    caches = ((cache_fox_k, cache_fox_v, cache_fox_logf), (cache_mla_ckv, cache_mla_krope),
              (cache_swa_k, cache_swa_v), (cache_sb_k, cache_sb_v))
    y_s, st_s = _trunk(x_sample, mods_s, caches, wts)
    fox_k_p, fox_v_p, fox_logf_p, mla_ckv_p, mla_krope_p, swa_k_p, swa_v_p, sb_k_p, sb_v_p = st_p
    fox_k_s, fox_v_s, fox_logf_s, mla_ckv_s, mla_krope_s, swa_k_s, swa_v_s, sb_k_s, sb_v_s = st_s
    return (y_p, y_s, fox_k_p, fox_k_s, fox_v_p, fox_v_s, fox_logf_p, fox_logf_s,
            mla_ckv_p, mla_ckv_s, mla_krope_p, mla_krope_s, swa_k_p, swa_k_s, swa_v_p, swa_v_s,
            sb_k_p, sb_k_s, sb_v_p, sb_v_s)
```

```python
import functools
import math

import jax
import jax.numpy as jnp
from jax import lax
from jax.experimental import pallas as pl
from jax.experimental.pallas import tpu as pltpu

F32 = jnp.float32
BF16 = jnp.bfloat16

D_MODEL = 1024
DEPTH = 4
CHUNK = 64
HEAD_DIM = 64
N_HEADS = 16
N_PAIRS = N_HEADS // 2
LANES = 128
MLA_Q_RANK = 384
MLA_KV_RANK = 256
MLA_D_NOPE = 64
MLA_D_ROPE = 32
ROPE_THETA = 10000.0
SWA_KV_HEADS = 4
SWA_GROUPS = 4
WINDOW = 128
N_EXPERTS = 16
D_EXPERT = 256
DEEPNORM_ALPHA = (2.0 * DEPTH) ** 0.25
NEG = -1e30
LOG2E = 1.4426950408889634
LONG_TQ = 512
VMEM_LIMIT = 56 * 1024 * 1024


def _cparams(*sem):
    return pltpu.CompilerParams(dimension_semantics=sem, vmem_limit_bytes=VMEM_LIMIT)


def _dot(a, b):
    return jnp.dot(a, b, preferred_element_type=F32)


def _dot_nt(a, b):
    return lax.dot_general(a, b, (((1,), (1,)), ((), ())), preferred_element_type=F32)


def _dot_f32(a, b):
    return lax.dot_general(a, b, (((1,), (0,)), ((), ())), precision=lax.Precision.HIGHEST,
                           preferred_element_type=F32)


def _sigmoid(x):
    return 1.0 / (1.0 + jnp.exp(-x))


def _log_sigmoid(x):
    return jnp.minimum(x, 0.0) - jnp.log1p(jnp.exp(-jnp.abs(x)))


def _softplus2(x):
    neg_abs = lax.bitcast_convert_type(lax.bitcast_convert_type(x, jnp.uint32) | jnp.uint32(1 << 31), F32)
    return jnp.maximum(x, 0.0) + jnp.log2(1.0 + jnp.exp2(neg_abs))


def _layer_norm(y, g, b):
    mu = jnp.mean(y, axis=-1, keepdims=True)
    yc = y - mu
    var = jnp.mean(yc * yc, axis=-1, keepdims=True)
    return yc * lax.rsqrt(var + 1e-5) * g + b


def _ada_kernel(c_ref, w_ref, b_ref, o_ref):
    c = c_ref[...]
    o_ref[0] = _dot_f32(c * _sigmoid(c), w_ref[0]) + b_ref[0]


def _ada_mod(c, ada_w, ada_b):
    n, d = c.shape
    depth, _, n6 = ada_w.shape
    tn = 1536
    return pl.pallas_call(
        _ada_kernel,
        grid=(depth, n6 // tn),
        in_specs=[pl.BlockSpec((n, d), lambda i, j: (0, 0)),
                  pl.BlockSpec((1, d, tn), lambda i, j: (i, 0, j)),
                  pl.BlockSpec((1, 1, tn), lambda i, j: (i, 0, j))],
        out_specs=pl.BlockSpec((1, n, tn), lambda i, j: (i, 0, j)),
        out_shape=jax.ShapeDtypeStruct((depth, n, n6), F32),
        compiler_params=_cparams("parallel", "parallel"),
        name="ada_mod",
    )(c, ada_w, ada_b.reshape(depth, 1, n6))


def _proj_kernel(*refs, n_w, emits, has_bias):
    x_ref, sc_ref, sh_ref = refs[:3]
    w_refs = refs[3:3 + n_w]
    pos = 3 + n_w
    bias_ref = refs[pos] if has_bias else None
    o_refs = refs[pos + (1 if has_bias else 0):]
    h = (x_ref[0] * (1.0 + sc_ref[0]) + sh_ref[0]).astype(BF16)
    oi = 0
    for j in range(n_w):
        y = _dot(h, w_refs[j][...])
        for kind in emits[j]:
            if kind == "logf":
                o_refs[oi][0] = _log_sigmoid(y + bias_ref[...])[:, :N_HEADS]
            else:
                o_refs[oi][0] = y.astype(kind)
            oi += 1


def _row_tile(s, target):
    return min(s, target)


def _proj(x, sc, sh, weights, emits, bias=None, tm=512, name="proj"):
    b, s, d = x.shape
    tm = _row_tile(s, tm)
    in_specs = [pl.BlockSpec((1, tm, d), lambda i, j: (i, j, 0)),
                pl.BlockSpec((1, 1, d), lambda i, j: (i, 0, 0)),
                pl.BlockSpec((1, 1, d), lambda i, j: (i, 0, 0))]
    args = [x, sc, sh]
    for w in weights:
        in_specs.append(pl.BlockSpec(w.shape, lambda i, j: (0, 0)))
        args.append(w)
    if bias is not None:
        in_specs.append(pl.BlockSpec(bias.shape, lambda i, j: (0, 0)))
        args.append(bias)
    out_specs, out_shape = [], []
    for w, em in zip(weights, emits):
        for kind in em:
            n, dt = (N_HEADS, F32) if kind == "logf" else (w.shape[1], kind)
            out_specs.append(pl.BlockSpec((1, tm, n), lambda i, j: (i, j, 0)))
            out_shape.append(jax.ShapeDtypeStruct((b, s, n), dt))
    return pl.pallas_call(
        functools.partial(_proj_kernel, n_w=len(weights), emits=emits, has_bias=bias is not None),
        grid=(b, s // tm),
        in_specs=in_specs, out_specs=out_specs, out_shape=out_shape,
        compiler_params=_cparams("parallel", "parallel"),
        name=name,
    )(*args)


FCOEF = LANES // N_HEADS


def _cumsum_kernel(x_ref, e_ref, pad_ref):
    nblk = x_ref.shape[1] // LANES
    r = lax.broadcasted_iota(jnp.int32, (LANES, LANES), 0)
    c = lax.broadcasted_iota(jnp.int32, (LANES, LANES), 1)
    tri = (c <= r).astype(F32)
    spread = [jnp.where(c == r * FCOEF + i, 1.0, 0.0).astype(BF16) for i in range(3)]
    ones = jnp.where((c % FCOEF) == 3, 1.0, 0.0)[0:1, :]
    pad_ref[...] = jnp.zeros_like(pad_ref)

    local = []
    for i in range(nblk):
        pad_ref[i, :, :N_HEADS] = x_ref[0, i * LANES:(i + 1) * LANES, :]
        local.append(_dot_f32(tri, pad_ref[i]))
    carry = jnp.zeros((1, LANES), F32)
    for i in range(nblk):
        f = local[i] + carry
        carry = f[LANES - 1:LANES, :]
        parts = _split3(f)
        e = ones + sum(_dot(parts[j].astype(BF16), spread[j]) for j in range(3))
        e_ref[0, i * LANES:(i + 1) * LANES, :] = e.astype(BF16)


def _forget_cumsum(logf):
    b, s, h = logf.shape
    return pl.pallas_call(
        _cumsum_kernel,
        grid=(b,),
        in_specs=[pl.BlockSpec((1, s, h), lambda i: (i, 0, 0))],
        out_specs=pl.BlockSpec((1, s, LANES), lambda i: (i, 0, 0)),
        out_shape=jax.ShapeDtypeStruct((b, s, LANES), BF16),
        scratch_shapes=[pltpu.VMEM((s // LANES, LANES, LANES), F32)],
        compiler_params=_cparams("parallel"),
        name="forget_cumsum",
    )(logf)


ROW_CHUNK = 128


def _half_mask(shape):
    return lax.broadcasted_iota(jnp.int32, shape, 1) < HEAD_DIM


def _loop(lo, hi, body, carry):
    if isinstance(lo, int) and isinstance(hi, int):
        for i in range(lo, hi):
            carry = body(i, carry)
        return carry
    return lax.fori_loop(lo, hi, body, carry)


def _split3(f):
    f1 = f.astype(BF16).astype(F32)
    r1 = f - f1
    f2 = r1.astype(BF16).astype(F32)
    return f1, f2, r1 - f2


def _forget_placement(pair):
    r = lax.broadcasted_iota(jnp.int32, (LANES, LANES), 0)
    c = lax.broadcasted_iota(jnp.int32, (LANES, LANES), 1)
    kmat = jnp.zeros((LANES, LANES), F32)
    qmat = jnp.zeros((LANES, LANES), F32)
    for hh in range(2):
        src = (pair * 2 + hh) * FCOEF
        base = HEAD_DIM if hh == 0 else 0
        i = r - src
        coef = (i >= 0) & (i < 3)
        one = i == 3
        kmat = kmat + jnp.where(coef & (c == base + 3 + i), -1.0, 0.0)
        kmat = kmat + jnp.where(one & (c >= base) & (c < base + 3), 1.0, 0.0)
        qmat = qmat + jnp.where(coef & (c == base + i), 1.0, 0.0)
        qmat = qmat + jnp.where(one & (c >= base + 3) & (c < base + 6), 1.0, 0.0)
    return kmat.astype(BF16), qmat.astype(BF16)


def _attn_kernel(*refs, kind, sq, skp, off, tq, tk, has_cache):
    it = iter(refs)
    q_ref, k_ref, v_ref = next(it), next(it), next(it)
    f_ref = next(it) if kind == "fox" else None
    ck_ref, cv_ref = (next(it), next(it)) if has_cache else (None, None)
    o_ref = next(it)
    kbuf, vbuf = (next(it), next(it)) if has_cache else (None, None)
    ka = (next(it), next(it)) if kind == "fox" else None
    va = (next(it), next(it)) if kind != "sb" else None

    if has_cache:
        past = ck_ref.shape[1]

        def fill(i, _):
            r0 = pl.multiple_of(i * ROW_CHUNK, ROW_CHUNK)
            kbuf[pl.ds(r0, ROW_CHUNK), :] = ck_ref[0, pl.ds(r0, ROW_CHUNK), :].astype(BF16)
            vbuf[pl.ds(r0, ROW_CHUNK), :] = cv_ref[0, pl.ds(r0, ROW_CHUNK), :].astype(BF16)
            return 0

        lax.fori_loop(0, past // ROW_CHUNK, fill, 0)
        kbuf[past:past + sq, :] = k_ref[0]
        vbuf[past:past + sq, :] = v_ref[0]
        if skp > past + sq:
            kbuf[past + sq:, :] = jnp.zeros((skp - past - sq, kbuf.shape[1]), BF16)
            vbuf[past + sq:, :] = jnp.zeros((skp - past - sq, LANES), BF16)
        kget = lambda r0, n: kbuf[pl.ds(r0, n), :]
        vget = lambda r0, n: vbuf[pl.ds(r0, n), :]
    else:
        kget = lambda r0, n: k_ref[0, pl.ds(r0, n), :]
        vget = lambda r0, n: v_ref[0, pl.ds(r0, n), :]

    if kind == "fox":
        kmat, qmat = _forget_placement(pl.program_id(1))
    if kind != "sb":
        lo_lanes = _half_mask((skp, LANES))
        vc = vget(0, skp)
        va[0][...] = jnp.where(lo_lanes, vc, jnp.ones_like(vc))
        va[1][...] = jnp.where(lo_lanes, jnp.ones_like(vc), vc)
        if kind == "fox":
            kc = kget(0, skp)
            e = _dot(f_ref[0], kmat).astype(BF16)
            ka[0][...] = jnp.where(lo_lanes, kc, e)
            ka[1][...] = jnp.where(lo_lanes, e, kc)

    nq = sq // tq
    row = lax.broadcasted_iota(jnp.int32, (tq, tk), 0)
    col = lax.broadcasted_iota(jnp.int32, (tq, tk), 1)
    lane_q = lax.broadcasted_iota(jnp.int32, (tq, LANES), 1)
    lo_half = lane_q < HEAD_DIM
    if kind == "sb":
        kr = lax.broadcasted_iota(jnp.int32, (2 * tk, tk), 0)
        kr = jnp.where(kr >= tk, kr - tk, kr)
        kc_ = lax.broadcasted_iota(jnp.int32, (2 * tk, tk), 1)
        later = jnp.where(kr > kc_, 1.0, 0.0).astype(BF16)

    def qblock(qi, _):
        q0 = qi * tq if isinstance(qi, int) else pl.multiple_of(qi * tq, tq)
        qblk = q_ref[0, pl.ds(q0, tq), :]
        qs = []
        if kind == "fox":
            eq = _dot(f_ref[0, pl.ds(off + q0, tq), :], qmat).astype(BF16)
        for hh in range(2):
            own = lo_half if hh == 0 else ~lo_half
            if kind == "mla":
                qs.append(qblk[:, hh * LANES:(hh + 1) * LANES])
            elif kind == "sb":
                qs.append(jnp.where(own, qblk, jnp.zeros_like(qblk)))
            else:
                qs.append(jnp.where(own, qblk, eq))

        def kload(hh, k0):
            if kind == "fox":
                return ka[hh][pl.ds(k0, tk), :]
            if kind == "mla":
                return kget(k0, tk)[:, hh * LANES:(hh + 1) * LANES]
            return kget(k0, tk)

        if kind == "sb":
            assert nq == 1
            nblk = skp // tk
            rowf = lax.broadcasted_iota(jnp.int32, (tq, skp), 0)
            colf = lax.broadcasted_iota(jnp.int32, (tq, skp), 1)
            vis = colf < (off + rowf)
            vall = vget(0, skp)
            outs = []
            for hh in range(2):
                z = _dot_nt(qs[hh], kget(0, skp))
                nl = jnp.where(vis, _softplus2(z), 0.0)
                hi = nl.astype(BF16)
                lo = (nl - hi.astype(F32)).astype(BF16)
                within = [_dot(jnp.concatenate([hi[:, j * tk:(j + 1) * tk], lo[:, j * tk:(j + 1) * tk]],
                                               axis=1), later) for j in range(nblk)]
                after = jnp.zeros((tq, 1), F32)
                parts = [None] * nblk
                for j in reversed(range(nblk)):
                    parts[j] = within[j] + after
                    after = after + within[j][:, 0:1] + nl[:, j * tk:j * tk + 1]
                suf = jnp.concatenate(parts, axis=1)
                a = jnp.where(vis, jnp.exp2(z - (nl + suf)), 0.0)
                outs.append(_dot(a.astype(BF16), vall))
        else:
            if kind == "fox":
                n_full = (off + q0 + 1) // tk
                n_total = (off + q0 + tq + tk - 1) // tk
            else:
                n_full = ((off + q0) // CHUNK * CHUNK + CHUNK) // tk
                n_total = (((off + q0 + tq - 1) // CHUNK + 1) * CHUNK + tk - 1) // tk

            def kstep(j, carry, masked):
                k0 = j * tk if isinstance(j, int) else pl.multiple_of(j * tk, tk)
                new = []
                for hh in range(2):
                    m, acc = carry[hh]
                    s = _dot_nt(qs[hh], kload(hh, k0))
                    if masked:
                        qpos = off + q0 + row
                        kpos = k0 + col
                        vis = (kpos <= qpos) if kind == "fox" else ((kpos // CHUNK) <= (qpos // CHUNK))
                        s = jnp.where(vis, s, NEG)
                    m_new = jnp.maximum(m, jnp.max(s, axis=-1, keepdims=True))
                    p = jnp.exp(s - m_new)
                    pv = _dot(p.astype(BF16), va[hh][pl.ds(k0, tk), :])
                    new.append((m_new, jnp.exp(m - m_new) * acc + pv))
                return tuple(new)

            init = ((jnp.full((tq, 1), NEG, F32), jnp.zeros((tq, LANES), F32)),) * 2
            carry = _loop(0, n_full, functools.partial(kstep, masked=False), init)
            carry = _loop(n_full, n_total, functools.partial(kstep, masked=True), carry)
            outs = [c[1] / pltpu.roll(c[1], HEAD_DIM, 1) for c in carry]
        o_ref[0, pl.ds(q0, tq), :] = jnp.where(lo_half, outs[0], outs[1]).astype(o_ref.dtype)
        return 0

    _loop(0, nq, qblock, 0) if nq == 1 else lax.fori_loop(0, nq, qblock, 0)


def _attention(kind, q, k, v, off, tq, tk, skp, f=None, cache=None):
    b, sq, qw = q.shape
    sn = k.shape[1]
    ql = qw // N_PAIRS
    pair = lambda n, w: pl.BlockSpec((1, n, w), lambda i, p: (i, 0, p))
    in_specs = [pair(sq, ql), pair(sn, ql), pair(sn, LANES)]
    args = [q, k, v]
    scratch = []
    if kind == "fox":
        in_specs.append(pl.BlockSpec((1, skp, LANES), lambda i, p: (i, 0, 0)))
        args.append(f)
    if cache is not None:
        ck, cv = cache
        in_specs += [pair(ck.shape[1], ql), pair(cv.shape[1], LANES)]
        args += [ck, cv]
        scratch += [pltpu.VMEM((skp, ql), BF16), pltpu.VMEM((skp, LANES), BF16)]
    else:
        assert skp == sn
    if kind == "fox":
        scratch += [pltpu.VMEM((skp, LANES), BF16)] * 2
    if kind != "sb":
        scratch += [pltpu.VMEM((skp, LANES), BF16)] * 2
    return pl.pallas_call(
        functools.partial(_attn_kernel, kind=kind, sq=sq, skp=skp, off=off, tq=tq, tk=tk,
                          has_cache=cache is not None),
        grid=(b, N_PAIRS),
        in_specs=in_specs,
        out_specs=pair(sq, LANES),
        out_shape=jax.ShapeDtypeStruct((b, sq, N_HEADS * HEAD_DIM), BF16),
        scratch_shapes=scratch,
        compiler_params=_cparams("parallel", "parallel"),
        name="attn_" + kind,
    )(*args)


def _attn_t_kernel(*refs, kind, sq, off, tq, tk):
    it = iter(refs)
    q_ref, k_ref, v_ref = next(it), next(it), next(it)
    f_ref = next(it) if kind == "fox" else None
    o_ref = next(it)
    ka = (next(it), next(it)) if kind == "fox" else None
    vt = (next(it), next(it)) if kind != "sb" else (next(it),)
    skp = k_ref.shape[1]

    lane_c = lax.broadcasted_iota(jnp.int32, (tk, LANES), 1)
    chan = lax.broadcasted_iota(jnp.int32, (LANES, tk), 0)
    if kind == "fox":
        kmat, qmat = _forget_placement(pl.program_id(1))

    def build(i, _):
        r0 = pl.multiple_of(i * tk, tk)
        vct = v_ref[0, pl.ds(r0, tk), :].astype(F32).T
        if kind == "sb":
            vt[0][:, pl.ds(r0, tk)] = vct.astype(BF16)
        else:
            vt[0][:, pl.ds(r0, tk)] = jnp.where(chan < HEAD_DIM, vct, 1.0).astype(BF16)
            vt[1][:, pl.ds(r0, tk)] = jnp.where(chan >= HEAD_DIM, vct, 1.0).astype(BF16)
        if kind == "fox":
            kc = k_ref[0, pl.ds(r0, tk), :]
            e = _dot(f_ref[0, pl.ds(r0, tk), :], kmat).astype(BF16)
            ka[0][pl.ds(r0, tk), :] = jnp.where(lane_c < HEAD_DIM, kc, e)
            ka[1][pl.ds(r0, tk), :] = jnp.where(lane_c >= HEAD_DIM, kc, e)
        return 0

    lax.fori_loop(0, skp // tk, build, 0)

    nq = sq // tq
    krow = lax.broadcasted_iota(jnp.int32, (tk, tq), 0)
    qcol = lax.broadcasted_iota(jnp.int32, (tk, tq), 1)
    lane_q = lax.broadcasted_iota(jnp.int32, (tq, LANES), 1)
    lo_half = lane_q < HEAD_DIM
    out_lo = lax.broadcasted_iota(jnp.int32, (LANES, tq), 0) < HEAD_DIM
    if kind == "sb":
        a_ = lax.broadcasted_iota(jnp.int32, (LANES, 2 * LANES), 0)
        b_ = lax.broadcasted_iota(jnp.int32, (LANES, 2 * LANES), 1)
        b_ = jnp.where(b_ >= LANES, b_ - LANES, b_)
        later_t = jnp.where(b_ > a_, 1.0, 0.0).astype(BF16)

    def qblock(qi):
        q0 = qi * tq
        qblk = q_ref[0, q0:q0 + tq, :]
        qs = []
        if kind == "fox":
            eq = _dot(f_ref[0, off + q0:off + q0 + tq, :], qmat).astype(BF16)
        for hh in range(2):
            own = lo_half if hh == 0 else ~lo_half
            if kind == "mla":
                qs.append(qblk[:, hh * LANES:(hh + 1) * LANES])
            elif kind == "sb":
                qs.append(jnp.where(own, qblk, jnp.zeros_like(qblk)))
            else:
                qs.append(jnp.where(own, qblk, eq))

        def kload(hh, k0):
            if kind == "fox":
                return ka[hh][k0:k0 + tk, :]
            if kind == "mla":
                return k_ref[0, k0:k0 + tk, hh * LANES:(hh + 1) * LANES]
            return k_ref[0, k0:k0 + tk, :]

        if kind == "sb":
            n_full = (off + q0) // tk
            n_total = (off + q0 + tq - 1 + tk - 1) // tk

            def stage1(j, masked):
                k0 = j * tk
                res = []
                for hh in range(2):
                    z = _dot_nt(kload(hh, k0), qs[hh])
                    nl = _softplus2(z)
                    if masked:
                        vis = (k0 + krow) < (off + q0 + qcol)
                        nl = jnp.where(vis, nl, 0.0)
                        z = jnp.where(vis, z, NEG)
                    hi = nl.astype(BF16)
                    res.append((z, nl, hi, (nl - hi.astype(F32)).astype(BF16)))
                return res

            def stage2(j, state, st1):
                vb = vt[0][:, j * tk:(j + 1) * tk]
                new = []
                for hh in range(2):
                    rem, acc = state[hh]
                    z, nl, hi, lo = st1[hh]
                    after = rem
                    parts = []
                    for sb in reversed(range(tk // LANES)):
                        rs = slice(sb * LANES, (sb + 1) * LANES)
                        within = _dot(later_t, jnp.concatenate([hi[rs], lo[rs]], axis=0))
                        parts.append(within + after)
                        after = after + within[0:1, :] + nl[sb * LANES:sb * LANES + 1, :]
                    suf = jnp.concatenate(parts[::-1], axis=0)
                    a = jnp.exp2(z - (nl + suf))
                    new.append((after, acc + _dot(vb, a.astype(BF16))))
                return new

            state = [(jnp.zeros((1, tq), F32), jnp.zeros((LANES, tq), F32))] * 2
            st1 = stage1(n_full, True)
            for j in range(n_full, -1, -1):
                nxt = stage1(j - 1, False) if j > 0 else None
                state = stage2(j, state, st1)
                st1 = nxt
            carry = state
            outs = [c[1] for c in carry]
        else:
            if kind == "fox":
                n_full = (off + q0 + 1) // tk
                n_total = (off + q0 + tq + tk - 1) // tk
            else:
                n_full = ((off + q0) // CHUNK * CHUNK + CHUNK) // tk
                n_total = (((off + q0 + tq - 1) // CHUNK + 1) * CHUNK + tk - 1) // tk

            def scores(j):
                return [_dot_nt(kload(hh, j * tk), qs[hh]) for hh in range(2)]

            s_cur = scores(0)
            m = [jnp.full((1, tq), NEG, F32)] * 2
            acc, p_prev, a_prev = [None, None], [None, None], [None, None]
            for j in range(n_total):
                s_next = scores(j + 1) if j + 1 < n_total else None
                for hh in range(2):
                    if j > 0:
                        pv = _dot(vt[hh][:, (j - 1) * tk:j * tk], p_prev[hh])
                        acc[hh] = pv if j == 1 else a_prev[hh] * acc[hh] + pv
                    s = s_cur[hh]
                    if j >= n_full:
                        qpos = off + q0 + qcol
                        kpos = j * tk + krow
                        vis = (kpos <= qpos) if kind == "fox" else ((kpos // CHUNK) <= (qpos // CHUNK))
                        s = jnp.where(vis, s, NEG)
                    m_new = jnp.maximum(m[hh], jnp.max(s, axis=0, keepdims=True))
                    p_prev[hh] = jnp.exp(s - m_new).astype(BF16)
                    a_prev[hh] = jnp.exp(m[hh] - m_new)
                    m[hh] = m_new
                s_cur = s_next
            accs = []
            for hh in range(2):
                pv = _dot(vt[hh][:, (n_total - 1) * tk:n_total * tk], p_prev[hh])
                accs.append(pv if n_total == 1 else a_prev[hh] * acc[hh] + pv)
            outs = [accs[0] / accs[0][HEAD_DIM:HEAD_DIM + 1, :], accs[1] / accs[1][0:1, :]]
        o_ref[0, q0:q0 + tq, :] = jnp.where(out_lo, outs[0], outs[1]).T.astype(o_ref.dtype)

    for qi in range(nq):
        qblock(qi)


def _attention_t(kind, q, k, v, off, tq, tk, f=None):
    b, sq, qw = q.shape
    skp = k.shape[1]
    ql = qw // N_PAIRS
    pair = lambda n, w: pl.BlockSpec((1, n, w), lambda i, p: (i, 0, p))
    in_specs = [pair(sq, ql), pair(skp, ql), pair(skp, LANES)]
    args = [q, k, v]
    scratch = []
    if kind == "fox":
        in_specs.append(pl.BlockSpec((1, skp, LANES), lambda i, p: (i, 0, 0)))
        args.append(f)
        scratch += [pltpu.VMEM((skp, LANES), BF16)] * 2
    scratch += [pltpu.VMEM((LANES, skp), BF16)] * (1 if kind == "sb" else 2)
    return pl.pallas_call(
        functools.partial(_attn_t_kernel, kind=kind, sq=sq, off=off, tq=tq, tk=tk),
        grid=(b, N_PAIRS),
        in_specs=in_specs,
        out_specs=pair(sq, LANES),
        out_shape=jax.ShapeDtypeStruct((b, sq, N_HEADS * HEAD_DIM), BF16),
        scratch_shapes=scratch,
        compiler_params=_cparams("parallel", "parallel"),
        name="attn_t_" + kind,
    )(*args)


SWA_TQ = 2 * CHUNK


def _swa_span(tq):
    return _round_up(tq + WINDOW, LANES)


def _swa_kernel(slope_ref, sink_ref, q_ref, k_ref, v_ref, o_ref, vta, vtb, *, sq, past, tq):
    nq = sq // tq
    kv = pl.program_id(1)
    sk = k_ref.shape[1]
    chan = lax.broadcasted_iota(jnp.int32, (LANES, ROW_CHUNK), 0)

    def build(i, _):
        r0 = pl.multiple_of(i * ROW_CHUNK, ROW_CHUNK)
        vct = v_ref[0, pl.ds(r0, ROW_CHUNK), :].astype(F32).T
        vta[:, pl.ds(r0, ROW_CHUNK)] = jnp.where(chan < HEAD_DIM, vct, 1.0).astype(BF16)
        vtb[:, pl.ds(r0, ROW_CHUNK)] = jnp.where(chan >= HEAD_DIM, vct, 1.0).astype(BF16)
        return 0

    lax.fori_loop(0, sk // ROW_CHUNK, build, 0)

    span = _swa_span(tq)
    krow = lax.broadcasted_iota(jnp.int32, (span, tq), 0)
    qcol = lax.broadcasted_iota(jnp.int32, (span, tq), 1)
    lo_half = _half_mask((tq, LANES))
    out_lo = lax.broadcasted_iota(jnp.int32, (LANES, tq), 0) < HEAD_DIM

    def qblocks(q0s):
        work = []
        for q0 in q0s:
            ks = jnp.maximum(q0 + past - WINDOW, 0)
            ks = ks if isinstance(ks, int) else pl.multiple_of(ks, LANES)
            kwin = k_ref[0, pl.ds(ks, span), :]
            for pair in range(SWA_GROUPS // 2):
                qpair = q_ref[0, pl.ds(q0, tq), pair * LANES:(pair + 1) * LANES]
                for half in range(2):
                    qh = jnp.where(lo_half if half == 0 else ~lo_half, qpair, jnp.zeros_like(qpair))
                    work.append(dict(q0=q0, ks=ks, pair=pair, half=half, s=_dot_nt(kwin, qh)))
        for w in work:
            q0, ks = w["q0"], w["ks"]
            head = kv * SWA_GROUPS + w["pair"] * 2 + w["half"]
            qpos = q0 + qcol
            kpos = ks - past + krow
            qc = qpos // CHUNK
            vis = (kpos >= qc * CHUNK - WINDOW) & (kpos < (qc + 1) * CHUNK)
            dist = jnp.abs(qpos - kpos).astype(F32)
            s = jnp.where(vis, w["s"] - slope_ref[head] * dist, NEG)
            sink = sink_ref[head]
            m = jnp.maximum(jnp.max(s, axis=0, keepdims=True), sink)
            w["e"] = jnp.exp(s - m).astype(BF16)
            w["tail"] = jnp.exp(sink - m)
        for w in work:
            vt = vta if w["half"] == 0 else vtb
            w["ot"] = _dot(vt[:, pl.ds(w["ks"], span)], w["e"])
        for i in range(0, len(work), 2):
            lo, hi = work[i], work[i + 1]
            o_lo = lo["ot"] / (lo["ot"][HEAD_DIM:HEAD_DIM + 1, :] + lo["tail"])
            o_hi = hi["ot"] / (hi["ot"][0:1, :] + hi["tail"])
            o_ref[0, pl.ds(lo["q0"], tq), lo["pair"] * LANES:(lo["pair"] + 1) * LANES] = (
                jnp.where(out_lo, o_lo, o_hi).T.astype(o_ref.dtype))

    if nq == 1:
        qblocks([0])
    else:
        def body(i, _):
            q0 = pl.multiple_of(i * (2 * tq), 2 * tq)
            qblocks([q0, q0 + tq])
            return 0

        lax.fori_loop(0, nq // 2, body, 0)


def _swa_attn(q, kdup, vdup, slopes, sinks, past, tq):
    b, sq, _ = q.shape
    sk = kdup.shape[1]
    gw = SWA_GROUPS * HEAD_DIM
    smem = pl.BlockSpec(memory_space=pltpu.SMEM)
    return pl.pallas_call(
        functools.partial(_swa_kernel, sq=sq, past=past, tq=tq),
        grid=(b, SWA_KV_HEADS),
        in_specs=[smem, smem,
                  pl.BlockSpec((1, sq, gw), lambda i, p: (i, 0, p)),
                  pl.BlockSpec((1, sk, LANES), lambda i, p: (i, 0, p)),
                  pl.BlockSpec((1, sk, LANES), lambda i, p: (i, 0, p))],
        out_specs=pl.BlockSpec((1, sq, gw), lambda i, p: (i, 0, p)),
        out_shape=jax.ShapeDtypeStruct((b, sq, N_HEADS * HEAD_DIM), BF16),
        scratch_shapes=[pltpu.VMEM((LANES, sk), BF16)] * 2,
        compiler_params=_cparams("parallel", "parallel"),
        name="attn_swa",
    )(slopes, sinks, q, kdup, vdup)


MLA_IN_COLS = MLA_Q_RANK + MLA_KV_RANK + 2 * LANES
MLA_HEAD_W = LANES


def _rms_norm(x, g):
    return x * lax.rsqrt(jnp.mean(x * x, axis=-1, keepdims=True) + 1e-6) * g


def _mla_proj_kernel(x_ref, sc_ref, sh_ref, win_ref, gq_ref, gkv_ref, wqa_ref, wqb_ref,
                     wk_ref, wv_ref, qc_ref, qs_ref, kc_ref, ks_ref,
                     q_ref, kf_ref, v_ref, ckv_ref, kr_ref):
    h = (x_ref[0] * (1.0 + sc_ref[0]) + sh_ref[0]).astype(BF16)
    proj = _dot(h, win_ref[...])
    cq = _rms_norm(proj[:, :MLA_Q_RANK], gq_ref[...]).astype(BF16)
    ckv = _rms_norm(proj[:, MLA_Q_RANK:MLA_Q_RANK + MLA_KV_RANK], gkv_ref[...])
    ckv_ref[0] = ckv
    ckv_b = ckv.astype(BF16)
    kr0 = MLA_Q_RANK + MLA_KV_RANK
    krr = proj[:, kr0:kr0 + LANES] * kc_ref[...] + proj[:, kr0 + LANES:kr0 + 2 * LANES] * ks_ref[...]
    kr_ref[0] = pltpu.roll(krr, LANES - MLA_D_NOPE, 1)[:, :MLA_D_ROPE]
    qa = _dot(cq, wqa_ref[...])
    qb = _dot(cq, wqb_ref[...])
    kn = _dot(ckv_b, wk_ref[...])
    v_ref[0] = _dot(ckv_b, wv_ref[...]).astype(BF16)
    qc, qs = qc_ref[...], qs_ref[...]
    for hd in range(N_HEADS):
        sl = slice(hd * MLA_HEAD_W, (hd + 1) * MLA_HEAD_W)
        q_ref[0, :, sl] = (qa[:, sl] * qc + qb[:, sl] * qs).astype(BF16)
        kf_ref[0, :, sl] = (kn[:, sl] + krr).astype(BF16)


def _mla_proj(x, sc, sh, mw, tables, tm=256):
    b, s, d = x.shape
    tm = _row_tile(s, tm)
    qc, qs, kc, ks = tables
    full = lambda a: pl.BlockSpec(a.shape, lambda i, j: (0,) * a.ndim)
    tab = pl.BlockSpec((tm, LANES), lambda i, j: (j, 0))
    row = lambda n: pl.BlockSpec((1, tm, n), lambda i, j: (i, j, 0))
    hw = N_HEADS * MLA_HEAD_W
    return pl.pallas_call(
        _mla_proj_kernel,
        grid=(b, s // tm),
        in_specs=[row(d),
                  pl.BlockSpec((1, 1, d), lambda i, j: (i, 0, 0)),
                  pl.BlockSpec((1, 1, d), lambda i, j: (i, 0, 0)),
                  full(mw["win"]), full(mw["gq"]), full(mw["gkv"]), full(mw["wqa"]),
                  full(mw["wqb"]), full(mw["wk"]), full(mw["wv"]), tab, tab, tab, tab],
        out_specs=[row(hw), row(hw), row(N_HEADS * HEAD_DIM), row(MLA_KV_RANK), row(MLA_D_ROPE)],
        out_shape=[jax.ShapeDtypeStruct((b, s, hw), BF16),
                   jax.ShapeDtypeStruct((b, s, hw), BF16),
                   jax.ShapeDtypeStruct((b, s, N_HEADS * HEAD_DIM), BF16),
                   jax.ShapeDtypeStruct((b, s, MLA_KV_RANK), F32),
                   jax.ShapeDtypeStruct((b, s, MLA_D_ROPE), F32)],
        compiler_params=_cparams("parallel", "parallel"),
        name="proj_mla",
    )(x, sc, sh, mw["win"], mw["gq"], mw["gkv"], mw["wqa"], mw["wqb"], mw["wk"], mw["wv"],
      qc, qs, kc, ks)


def _mla_cache_kernel(ckv_ref, kr_ref, wk_ref, wv_ref, place_ref, kf_ref, v_ref):
    ckv_b = ckv_ref[0].astype(BF16)
    kf_ref[0] = (_dot(ckv_b, wk_ref[...]) + _dot(kr_ref[0], place_ref[...])).astype(BF16)
    v_ref[0] = _dot(ckv_b, wv_ref[...]).astype(BF16)


def _mla_cache_up(ckv, kr_pad, mw, tm=512):
    b, p, r = ckv.shape
    tm = _row_tile(p, tm)
    hw = N_HEADS * MLA_HEAD_W
    full = lambda a: pl.BlockSpec(a.shape, lambda i, j: (0,) * a.ndim)
    row = lambda n: pl.BlockSpec((1, tm, n), lambda i, j: (i, j, 0))
    return pl.pallas_call(
        _mla_cache_kernel,
        grid=(b, p // tm),
        in_specs=[row(r), row(LANES), full(mw["wk"]), full(mw["wv"]), full(mw["place"])],
        out_specs=[row(hw), row(N_HEADS * HEAD_DIM)],
        out_shape=[jax.ShapeDtypeStruct((b, p, hw), BF16),
                   jax.ShapeDtypeStruct((b, p, N_HEADS * HEAD_DIM), BF16)],
        compiler_params=_cparams("parallel", "parallel"),
        name="mla_cache_up",
    )(ckv, kr_pad, mw["wk"], mw["wv"], mw["place"])


def _xor_partner(x, k, idx, axis):
    n = x.shape[axis]
    up = pltpu.roll(x, n - k, axis)
    dn = pltpu.roll(x, k, axis)
    return jnp.where((idx & k) == 0, up, dn)


def _argmax_groups(v, idx, pos, strides, axis):
    for k in strides:
        pv = _xor_partner(v, k, pos, axis)
        pi = _xor_partner(idx, k, pos, axis)
        take = (pv > v) | ((pv == v) & (pi < idx))
        v = jnp.where(take, pv, v)
        idx = jnp.where(take, pi, idx)
    return v, idx


def _route(logits, rb):
    ex = lax.broadcasted_iota(jnp.int32, logits.shape, 0)
    scores = _sigmoid(logits)
    biased = scores + rb
    p1 = _xor_partner(biased, 1, ex, 0)
    hi1, lo1 = jnp.maximum(biased, p1), jnp.minimum(biased, p1)
    hi2, lo2 = _xor_partner(hi1, 2, ex, 0), _xor_partner(lo1, 2, ex, 0)
    group_score = jnp.maximum(hi1, hi2) + jnp.maximum(jnp.minimum(hi1, hi2), jnp.maximum(lo1, lo2))
    gid = ex >> 2
    _, best = _argmax_groups(group_score, gid, ex, (4, 8), 0)
    cand = jnp.where(gid == best, biased, NEG)
    _, i1 = _argmax_groups(cand, ex, ex, (1, 2, 4, 8), 0)
    _, i2 = _argmax_groups(jnp.where(ex == i1, -jnp.inf, cand), ex, ex, (1, 2, 4, 8), 0)
    sel = jnp.where((ex == i1) | (ex == i2), scores, 0.0)
    return sel / jnp.sum(sel, axis=0, keepdims=True)


def _outproj_kernel(o_ref, x_ref, w_ref, ga_ref, scf_ref, shf_ref, g_ref, b_ref,
                    rwh_ref, rwl_ref, rb_ref, x1_ref, h2_ref, gate_ref):
    y = DEEPNORM_ALPHA * x_ref[0] + (1.0 + ga_ref[0]) * _dot(o_ref[0], w_ref[...])
    x1 = _layer_norm(y, g_ref[...], b_ref[...])
    x1_ref[0] = x1
    h2 = x1 * (1.0 + scf_ref[0]) + shf_ref[0]
    hi = h2.astype(BF16)
    h2_ref[0] = hi
    lo = (h2 - hi.astype(F32)).astype(BF16)
    logits = _dot_nt(rwh_ref[...], hi) + _dot_nt(rwh_ref[...], lo) + _dot_nt(rwl_ref[...], hi)
    gates = _route(logits, rb_ref[...])
    pad = jnp.zeros((LANES - N_EXPERTS, gates.shape[1]), F32)
    gate_ref[0] = jnp.concatenate([gates, pad], axis=0).T[:, :N_EXPERTS]


def _outproj(o, x, w, ga, scf, shf, g, bta, rwh, rwl, rb, tm=512):
    b, s, d = x.shape
    tm = _row_tile(s, tm)
    row = lambda n: pl.BlockSpec((1, tm, n), lambda i, j: (i, j, 0))
    mod = pl.BlockSpec((1, 1, d), lambda i, j: (i, 0, 0))
    full = lambda a: pl.BlockSpec(a.shape, lambda i, j: (0,) * a.ndim)
    return pl.pallas_call(
        _outproj_kernel,
        grid=(b, s // tm),
        in_specs=[row(d), row(d), full(w), mod, mod, mod, full(g), full(bta),
                  full(rwh), full(rwl), full(rb)],
        out_specs=[row(d), row(d), row(N_EXPERTS)],
        out_shape=[jax.ShapeDtypeStruct((b, s, d), F32), jax.ShapeDtypeStruct((b, s, d), BF16),
                   jax.ShapeDtypeStruct((b, s, N_EXPERTS), F32)],
        compiler_params=_cparams("parallel", "parallel"),
        name="outproj_ln_router",
    )(o, x, w, ga, scf, shf, g, bta, rwh, rwl, rb)


EXPERT_UNROLL = 4


def _moe_kernel(h_ref, gate_ref, x_ref, gf_ref, g_ref, b_ref, wg_ref, wu_ref, wd_ref,
                o_ref, acc_ref):
    bt, tm, d = h_ref.shape
    rows = bt * tm
    ne, _, de = wg_ref.shape
    h = h_ref[...].reshape(rows, d)
    gates = gate_ref[...].reshape(rows, N_EXPERTS)
    lane = lax.broadcasted_iota(jnp.int32, gates.shape, 1)
    acc_ref[...] = jnp.zeros_like(acc_ref)

    def body(i, _):
        e0 = i * EXPERT_UNROLL
        acts = []
        for u in range(EXPERT_UNROLL):
            e = e0 + u
            hg = _dot(h, wg_ref[e])
            hu = _dot(h, wu_ref[e])
            ge = jnp.sum(jnp.where(lane == e, gates, 0.0), axis=-1, keepdims=True)
            acts.append((hg * _sigmoid(hg) * hu * ge).astype(BF16))
        wd = wd_ref[pl.ds(e0, EXPERT_UNROLL)].reshape(EXPERT_UNROLL * de, d)
        acc_ref[...] += _dot(jnp.concatenate(acts, axis=1), wd)
        return 0

    lax.fori_loop(0, ne // EXPERT_UNROLL, body, 0)
    y = DEEPNORM_ALPHA * x_ref[...] + (1.0 + gf_ref[...]) * acc_ref[...].reshape(bt, tm, d)
    o_ref[...] = _layer_norm(y, g_ref[...], b_ref[...])


def _moe(h2, gates, x1, gf, g, bta, wg, wu, wd, rows=512):
    b, s, d = x1.shape
    tm = _row_tile(s, rows)
    bt = min(b, rows // tm)
    row = lambda n: pl.BlockSpec((bt, tm, n), lambda i, j: (i, j, 0))
    full = lambda a: pl.BlockSpec(a.shape, lambda i, j: (0,) * a.ndim)
    once = lambda a: pl.BlockSpec(a.shape, lambda i, j: (0,) * a.ndim, pipeline_mode=pl.Buffered(1))
    return pl.pallas_call(
        _moe_kernel,
        grid=(b // bt, s // tm),
        in_specs=[row(d), row(N_EXPERTS), row(d),
                  pl.BlockSpec((bt, 1, d), lambda i, j: (i, 0, 0)), full(g), full(bta),
                  once(wg), once(wu), once(wd)],
        out_specs=row(d),
        out_shape=jax.ShapeDtypeStruct((b, s, d), F32),
        scratch_shapes=[pltpu.VMEM((bt * tm, d), F32)],
        compiler_params=_cparams("parallel", "parallel"),
        name="moe_ffn",
    )(h2, gates, x1, gf, g, bta, wg, wu, wd)


def _pad_cols(w, n):
    return jnp.pad(w, ((0, 0), (0, n - w.shape[1])))


def _rot_half_cols(w):
    half = w.shape[-1] // 2
    return jnp.concatenate([-w[..., half:], w[..., :half]], axis=-1)


def _mla_weights(mla_w_in, mla_q_norm, mla_w_uq, mla_kv_norm, mla_w_uk, mla_w_uv):
    d = mla_w_in.shape[0]
    kr0 = MLA_Q_RANK + MLA_KV_RANK
    wkr = mla_w_in[:, kr0:]
    z = lambda n: jnp.zeros((d, n), F32)
    win = jnp.concatenate([mla_w_in[:, :kr0], z(MLA_D_NOPE), wkr, z(LANES - MLA_D_NOPE - MLA_D_ROPE),
                           z(MLA_D_NOPE), _rot_half_cols(wkr), z(LANES - MLA_D_NOPE - MLA_D_ROPE)],
                          axis=1)
    wuq = mla_w_uq.reshape(MLA_Q_RANK, N_HEADS, MLA_D_NOPE + MLA_D_ROPE)
    zq = lambda n: jnp.zeros((MLA_Q_RANK, N_HEADS, n), F32)
    tail = LANES - MLA_D_NOPE - MLA_D_ROPE
    wqa = jnp.concatenate([wuq, zq(tail)], axis=-1).reshape(MLA_Q_RANK, N_HEADS * LANES)
    wqb = jnp.concatenate([zq(MLA_D_NOPE), _rot_half_cols(wuq[..., MLA_D_NOPE:]), zq(tail)],
                          axis=-1).reshape(MLA_Q_RANK, N_HEADS * LANES)
    wk = jnp.concatenate([mla_w_uk, jnp.zeros((MLA_KV_RANK, N_HEADS, LANES - MLA_D_NOPE), F32)],
                         axis=-1).reshape(MLA_KV_RANK, N_HEADS * LANES)
    wv = mla_w_uv.reshape(MLA_KV_RANK, N_HEADS * HEAD_DIM)
    src = jnp.arange(LANES)[:, None]
    dst = jnp.arange(N_HEADS * LANES)[None, :]
    place = ((dst % LANES) == src + MLA_D_NOPE) & (src < MLA_D_ROPE)
    return dict(win=win.astype(BF16), gq=mla_q_norm.reshape(1, -1), gkv=mla_kv_norm.reshape(1, -1),
                wqa=wqa.astype(BF16), wqb=wqb.astype(BF16), wk=wk.astype(BF16),
                wv=wv.astype(BF16), place=place.astype(BF16))


def _rope_tables(n_pos):
    half = MLA_D_ROPE // 2
    inv = ROPE_THETA ** (-jnp.arange(half, dtype=F32) * 2.0 / MLA_D_ROPE)
    ang = jnp.arange(n_pos, dtype=F32)[:, None] * inv[None, :]
    cos = jnp.concatenate([jnp.cos(ang), jnp.cos(ang)], axis=-1)
    sin = jnp.concatenate([jnp.sin(ang), jnp.sin(ang)], axis=-1)
    scale = (MLA_D_NOPE + MLA_D_ROPE) ** -0.5
    z = lambda n: jnp.zeros((n_pos, n), F32)
    tail = LANES - MLA_D_NOPE - MLA_D_ROPE
    qc = jnp.concatenate([jnp.full((n_pos, MLA_D_NOPE), scale, F32), cos * scale, z(tail)], axis=1)
    qs = jnp.concatenate([z(MLA_D_NOPE), sin * scale, z(tail)], axis=1)
    kc = jnp.concatenate([z(MLA_D_NOPE), cos, z(tail)], axis=1)
    ks = jnp.concatenate([z(MLA_D_NOPE), sin, z(tail)], axis=1)
    return qc, qs, kc, ks


def _dup_heads(w):
    d = w.shape[0]
    w4 = w.reshape(d, SWA_KV_HEADS, HEAD_DIM)
    return jnp.concatenate([w4, w4], axis=-1).reshape(d, SWA_KV_HEADS * LANES)


def _pad_rows(a, n):
    return jnp.pad(a, ((0, 0), (0, n - a.shape[1]), (0, 0)))


def _round_up(n, m):
    return -(-n // m) * m


def _attn_tiles(kind, sq, n_keys):
    if sq % LONG_TQ == 0:
        return LONG_TQ, LONG_TQ, n_keys
    if kind == "sb":
        return sq, 256, _round_up(n_keys, 256)
    skp = _round_up(n_keys, LANES)
    return sq, skp, skp


def _flat_heads(a):
    return a.reshape(a.shape[0], a.shape[1], -1)


def _fox_mixer(x, sc, sh, fw, cache):
    b, s, _ = x.shape
    q, k32, kb, v32, vb, logf = _proj(x, sc, sh, fw["w"], [[BF16], [F32, BF16], [F32, BF16], ["logf"]],
                                      bias=fw["bias"], name="proj_fox")
    off = 0 if cache is None else cache[0].shape[1]
    tq, tk, skp = _attn_tiles("fox", s, off + s)
    lall = logf if cache is None else jnp.concatenate([cache[2], logf], axis=1)
    f = _forget_cumsum(_pad_rows(lall, skp))
    kv_cache = None if cache is None else (_flat_heads(cache[0]), _flat_heads(cache[1]))
    if cache is None:
        o = _attention_t("fox", q, kb, vb, off, tq, tk, f=f)
    else:
        o = _attention("fox", q, kb, vb, off, tq, tk, skp, f=f, cache=kv_cache)
    hshape = (b, s, N_HEADS, HEAD_DIM)
    return o, (k32.reshape(hshape), v32.reshape(hshape), logf)


def _mla_mixer(x, sc, sh, mw, cache):
    b, s, _ = x.shape
    off = 0 if cache is None else cache[0].shape[1]
    tables = [t[off:off + s] for t in mw["tables"]]
    q, kf, v, ckv, kr = _mla_proj(x, sc, sh, mw, tables)
    tq, tk, skp = _attn_tiles("mla", s, off + s)
    kv_cache = None
    if cache is not None:
        c_ckv, c_kr = cache
        kr_pad = jnp.pad(c_kr, ((0, 0), (0, 0), (0, LANES - MLA_D_ROPE))).astype(BF16)
        kv_cache = _mla_cache_up(c_ckv, kr_pad, mw)
    if cache is None:
        o = _attention_t("mla", q, kf, v, off, tq, tk)
    else:
        o = _attention("mla", q, kf, v, off, tq, tk, skp, cache=kv_cache)
    return o, (ckv, kr)


def _swa_mixer(x, sc, sh, sw, cache):
    b, s, _ = x.shape
    q, k32, kdup, v32, vdup = _proj(x, sc, sh, sw["w"], [[BF16], [F32], [BF16], [F32], [BF16]],
                                    name="proj_swa")
    kvshape = lambda a: a.reshape(a.shape[0], a.shape[1], SWA_KV_HEADS, HEAD_DIM)
    if cache is None:
        past, tq = 0, min(s, SWA_TQ)
        new_k, new_v = kvshape(k32[:, -WINDOW:]), kvshape(v32[:, -WINDOW:])
    else:
        ck, cv = cache
        past, tq = ck.shape[1], s
        dup = lambda c: jnp.concatenate([c, c], axis=-1).reshape(c.shape[0], c.shape[1], -1).astype(BF16)
        kdup = _pad_rows(jnp.concatenate([dup(ck), kdup], axis=1), _swa_span(tq))
        vdup = _pad_rows(jnp.concatenate([dup(cv), vdup], axis=1), _swa_span(tq))
        new_k = jnp.concatenate([ck, kvshape(k32)], axis=1)[:, -WINDOW:]
        new_v = jnp.concatenate([cv, kvshape(v32)], axis=1)[:, -WINDOW:]
    o = _swa_attn(q, kdup, vdup, sw["slopes"], sw["sinks"], past, tq)
    return o, (new_k, new_v)


def _sb_mixer(x, sc, sh, bw, cache):
    b, s, _ = x.shape
    q, k32, kb, v32, vb = _proj(x, sc, sh, bw["w"], [[BF16], [F32, BF16], [F32, BF16]], name="proj_sb")
    off = 0 if cache is None else cache[0].shape[1]
    tq, tk, skp = _attn_tiles("sb", s, off + s)
    kv_cache = None if cache is None else (_flat_heads(cache[0]), _flat_heads(cache[1]))
    if cache is None:
        o = _attention_t("sb", q, kb, vb, off, tq, tk)
    else:
        o = _attention("sb", q, kb, vb, off, tq, tk, skp, cache=kv_cache)
    hshape = (b, s, N_HEADS, HEAD_DIM)
    return o, (k32.reshape(hshape), v32.reshape(hshape))


_MIXERS = (_fox_mixer, _mla_mixer, _swa_mixer, _sb_mixer)


def _trunk(x, mods, caches, wts):
    states = []
    for i in range(DEPTH):
        m = [mods[i, :, j][:, None, :] for j in range(6)]
        sh_a, sc_a, g_a, sh_f, sc_f, g_f = m
        o, st = _MIXERS[i](x, sc_a, sh_a, wts["mixer"][i], caches[i])
        states.extend(st)
        x1, h2, gates = _outproj(o, x, wts["w_out"][i], g_a, sc_f, sh_f,
                                 wts["ln_g"][i, 0:1], wts["ln_b"][i, 0:1],
                                 wts["rw_hi"], wts["rw_lo"], wts["rb"])
        x = _moe(h2, gates, x1, g_f, wts["ln_g"][i, 1:2], wts["ln_b"][i, 1:2],
                 wts["moe_g"][i], wts["moe_u"][i], wts["moe_d"][i])
    return x, states


def kernel(x_prompt, x_sample, cache_fox_k, cache_fox_v, cache_fox_logf, cache_mla_ckv, cache_mla_krope, cache_swa_k, cache_swa_v, cache_sb_k, cache_sb_v, c_prompt, c_sample, ada_w, ada_b, ln_g, ln_b, fox_w_in, fox_b_f, fox_w_out, mla_w_in, mla_q_norm, mla_w_uq, mla_kv_norm, mla_w_uk, mla_w_uv, mla_w_out, swa_w_in, swa_sinks, swa_w_out, sb_w_in, sb_w_out, router_w, router_b, moe_w_gate, moe_w_up, moe_w_down):
    bp, sp, d = x_prompt.shape
    bs, ss, _ = x_sample.shape
    past = cache_fox_k.shape[1]
    hw = N_HEADS * HEAD_DIM
    qscale = HEAD_DIM ** -0.5

    fox = dict(w=[(fox_w_in[:, :hw] * qscale).astype(BF16), fox_w_in[:, hw:2 * hw].astype(BF16),
                  fox_w_in[:, 2 * hw:3 * hw].astype(BF16), _pad_cols(fox_w_in[:, 3 * hw:], LANES).astype(BF16)],
               bias=_pad_cols(fox_b_f.reshape(1, -1), LANES))
    mla = _mla_weights(mla_w_in, mla_q_norm, mla_w_uq, mla_kv_norm, mla_w_uk, mla_w_uv)
    mla["tables"] = _rope_tables(max(sp, past + ss))
    kvw = SWA_KV_HEADS * HEAD_DIM
    wk, wv = swa_w_in[:, hw:hw + kvw], swa_w_in[:, hw + kvw:]
    swa = dict(w=[(swa_w_in[:, :hw] * qscale).astype(BF16), wk.astype(BF16), _dup_heads(wk).astype(BF16),
                  wv.astype(BF16), _dup_heads(wv).astype(BF16)],
               slopes=jnp.exp2(-8.0 * jnp.arange(1, N_HEADS + 1, dtype=F32) / N_HEADS),
               sinks=swa_sinks.astype(F32))
    sb = dict(w=[(sb_w_in[:, :hw] * (qscale * LOG2E)).astype(BF16), sb_w_in[:, hw:2 * hw].astype(BF16),
                 sb_w_in[:, 2 * hw:].astype(BF16)])
    rw = router_w.T
    rw_hi = rw.astype(BF16)
    wts = dict(mixer=(fox, mla, swa, sb),
               w_out=[w.astype(BF16) for w in (fox_w_out, mla_w_out, swa_w_out, sb_w_out)],
               ln_g=ln_g, ln_b=ln_b, rw_hi=rw_hi, rw_lo=(rw - rw_hi.astype(F32)).astype(BF16),
               rb=router_b.reshape(-1, 1),
               moe_g=moe_w_gate.astype(BF16), moe_u=moe_w_up.astype(BF16), moe_d=moe_w_down.astype(BF16))

    mods = _ada_mod(jnp.concatenate([c_prompt, c_sample], axis=0), ada_w, ada_b)
    mods_p = mods[:, :bp].reshape(DEPTH, bp, 6, d)
    mods_s = mods[:, bp:].reshape(DEPTH, bs, 6, d)

    y_p, st_p = _trunk(x_prompt, mods_p, (None, None, None, None), wts)
    caches = ((cache_fox_k, cache_fox_v, cache_fox_logf), (cache_mla_ckv, cache_mla_krope),
              (cache_swa_k, cache_swa_v), (cache_sb_k, cache_sb_v))
    y_s, st_s = _trunk(x_sample, mods_s, caches, wts)
    fox_k_p, fox_v_p, fox_logf_p, mla_ckv_p, mla_krope_p, swa_k_p, swa_v_p, sb_k_p, sb_v_p = st_p
    fox_k_s, fox_v_s, fox_logf_s, mla_ckv_s, mla_krope_s, swa_k_s, swa_v_s, sb_k_s, sb_v_s = st_s
    return (y_p, y_s, fox_k_p, fox_k_s, fox_v_p, fox_v_s, fox_logf_p, fox_logf_s,
            mla_ckv_p, mla_ckv_s, mla_krope_p, mla_krope_s, swa_k_p, swa_k_s, swa_v_p, swa_v_s,
            sb_k_p, sb_k_s, sb_v_p, sb_v_s)
```

```python
import functools
import math

import jax
import jax.numpy as jnp
from jax import lax
from jax.experimental import pallas as pl
from jax.experimental.pallas import tpu as pltpu

F32 = jnp.float32
BF16 = jnp.bfloat16

D_MODEL = 1024
DEPTH = 4
CHUNK = 64
HEAD_DIM = 64
N_HEADS = 16
N_PAIRS = N_HEADS // 2
LANES = 128
MLA_Q_RANK = 384
MLA_KV_RANK = 256
MLA_D_NOPE = 64
MLA_D_ROPE = 32
ROPE_THETA = 10000.0
SWA_KV_HEADS = 4
SWA_GROUPS = 4
WINDOW = 128
N_EXPERTS = 16
D_EXPERT = 256
DEEPNORM_ALPHA = (2.0 * DEPTH) ** 0.25
NEG = -1e30
LOG2E = 1.4426950408889634
LONG_TQ = 512
VMEM_LIMIT = 56 * 1024 * 1024


def _cparams(*sem):
    return pltpu.CompilerParams(dimension_semantics=sem, vmem_limit_bytes=VMEM_LIMIT)


def _dot(a, b):
    return jnp.dot(a, b, preferred_element_type=F32)


def _dot_nt(a, b):
    return lax.dot_general(a, b, (((1,), (1,)), ((), ())), preferred_element_type=F32)


def _dot_f32(a, b):
    return lax.dot_general(a, b, (((1,), (0,)), ((), ())), precision=lax.Precision.HIGHEST,
                           preferred_element_type=F32)


def _sigmoid(x):
    return 1.0 / (1.0 + jnp.exp(-x))


def _log_sigmoid(x):
    return jnp.minimum(x, 0.0) - jnp.log1p(jnp.exp(-jnp.abs(x)))


def _softplus2(x):
    neg_abs = lax.bitcast_convert_type(lax.bitcast_convert_type(x, jnp.uint32) | jnp.uint32(1 << 31), F32)
    return jnp.maximum(x, 0.0) + jnp.log2(1.0 + jnp.exp2(neg_abs))


def _layer_norm(y, g, b):
    mu = jnp.mean(y, axis=-1, keepdims=True)
    yc = y - mu
    var = jnp.mean(yc * yc, axis=-1, keepdims=True)
    return yc * lax.rsqrt(var + 1e-5) * g + b


def _ada_kernel(c_ref, w_ref, b_ref, o_ref):
    c = c_ref[...]
    o_ref[0] = _dot_f32(c * _sigmoid(c), w_ref[0]) + b_ref[0]


def _ada_mod(c, ada_w, ada_b):
    n, d = c.shape
    depth, _, n6 = ada_w.shape
    tn = 1536
    return pl.pallas_call(
        _ada_kernel,
        grid=(depth, n6 // tn),
        in_specs=[pl.BlockSpec((n, d), lambda i, j: (0, 0)),
                  pl.BlockSpec((1, d, tn), lambda i, j: (i, 0, j)),
                  pl.BlockSpec((1, 1, tn), lambda i, j: (i, 0, j))],
        out_specs=pl.BlockSpec((1, n, tn), lambda i, j: (i, 0, j)),
        out_shape=jax.ShapeDtypeStruct((depth, n, n6), F32),
        compiler_params=_cparams("parallel", "parallel"),
        name="ada_mod",
    )(c, ada_w, ada_b.reshape(depth, 1, n6))


def _proj_kernel(*refs, n_w, emits, has_bias):
    x_ref, sc_ref, sh_ref = refs[:3]
    w_refs = refs[3:3 + n_w]
    pos = 3 + n_w
    bias_ref = refs[pos] if has_bias else None
    o_refs = refs[pos + (1 if has_bias else 0):]
    h = (x_ref[0] * (1.0 + sc_ref[0]) + sh_ref[0]).astype(BF16)
    oi = 0
    for j in range(n_w):
        y = _dot(h, w_refs[j][...])
        for kind in emits[j]:
            if kind == "logf":
                o_refs[oi][0] = _log_sigmoid(y + bias_ref[...])[:, :N_HEADS]
            else:
                o_refs[oi][0] = y.astype(kind)
            oi += 1


def _row_tile(s, target):
    return min(s, target)


def _proj(x, sc, sh, weights, emits, bias=None, tm=512, name="proj"):
    b, s, d = x.shape
    tm = _row_tile(s, tm)
    in_specs = [pl.BlockSpec((1, tm, d), lambda i, j: (i, j, 0)),
                pl.BlockSpec((1, 1, d), lambda i, j: (i, 0, 0)),
                pl.BlockSpec((1, 1, d), lambda i, j: (i, 0, 0))]
    args = [x, sc, sh]
    for w in weights:
        in_specs.append(pl.BlockSpec(w.shape, lambda i, j: (0, 0)))
        args.append(w)
    if bias is not None:
        in_specs.append(pl.BlockSpec(bias.shape, lambda i, j: (0, 0)))
        args.append(bias)
    out_specs, out_shape = [], []
    for w, em in zip(weights, emits):
        for kind in em:
            n, dt = (N_HEADS, F32) if kind == "logf" else (w.shape[1], kind)
            out_specs.append(pl.BlockSpec((1, tm, n), lambda i, j: (i, j, 0)))
            out_shape.append(jax.ShapeDtypeStruct((b, s, n), dt))
    return pl.pallas_call(
        functools.partial(_proj_kernel, n_w=len(weights), emits=emits, has_bias=bias is not None),
        grid=(b, s // tm),
        in_specs=in_specs, out_specs=out_specs, out_shape=out_shape,
        compiler_params=_cparams("parallel", "parallel"),
        name=name,
    )(*args)


FCOEF = LANES // N_HEADS


def _cumsum_kernel(x_ref, e_ref, pad_ref):
    nblk = x_ref.shape[1] // LANES
    r = lax.broadcasted_iota(jnp.int32, (LANES, LANES), 0)
    c = lax.broadcasted_iota(jnp.int32, (LANES, LANES), 1)
    tri = (c <= r).astype(F32)
    spread = [jnp.where(c == r * FCOEF + i, 1.0, 0.0).astype(BF16) for i in range(3)]
    ones = jnp.where((c % FCOEF) == 3, 1.0, 0.0)[0:1, :]
    pad_ref[...] = jnp.zeros_like(pad_ref)

    local = []
    for i in range(nblk):
        pad_ref[i, :, :N_HEADS] = x_ref[0, i * LANES:(i + 1) * LANES, :]
        local.append(_dot_f32(tri, pad_ref[i]))
    carry = jnp.zeros((1, LANES), F32)
    for i in range(nblk):
        f = local[i] + carry
        carry = f[LANES - 1:LANES, :]
        parts = _split3(f)
        e = ones + sum(_dot(parts[j].astype(BF16), spread[j]) for j in range(3))
        e_ref[0, i * LANES:(i + 1) * LANES, :] = e.astype(BF16)


def _forget_cumsum(logf):
    b, s, h = logf.shape
    return pl.pallas_call(
        _cumsum_kernel,
        grid=(b,),
        in_specs=[pl.BlockSpec((1, s, h), lambda i: (i, 0, 0))],
        out_specs=pl.BlockSpec((1, s, LANES), lambda i: (i, 0, 0)),
        out_shape=jax.ShapeDtypeStruct((b, s, LANES), BF16),
        scratch_shapes=[pltpu.VMEM((s // LANES, LANES, LANES), F32)],
        compiler_params=_cparams("parallel"),
        name="forget_cumsum",
    )(logf)


ROW_CHUNK = 128


def _half_mask(shape):
    return lax.broadcasted_iota(jnp.int32, shape, 1) < HEAD_DIM


def _loop(lo, hi, body, carry):
    if isinstance(lo, int) and isinstance(hi, int):
        for i in range(lo, hi):
            carry = body(i, carry)
        return carry
    return lax.fori_loop(lo, hi, body, carry)


def _split3(f):
    f1 = f.astype(BF16).astype(F32)
    r1 = f - f1
    f2 = r1.astype(BF16).astype(F32)
    return f1, f2, r1 - f2


def _forget_placement(pair):
    r = lax.broadcasted_iota(jnp.int32, (LANES, LANES), 0)
    c = lax.broadcasted_iota(jnp.int32, (LANES, LANES), 1)
    kmat = jnp.zeros((LANES, LANES), F32)
    qmat = jnp.zeros((LANES, LANES), F32)
    for hh in range(2):
        src = (pair * 2 + hh) * FCOEF
        base = HEAD_DIM if hh == 0 else 0
        i = r - src
        coef = (i >= 0) & (i < 3)
        one = i == 3
        kmat = kmat + jnp.where(coef & (c == base + 3 + i), -1.0, 0.0)
        kmat = kmat + jnp.where(one & (c >= base) & (c < base + 3), 1.0, 0.0)
        qmat = qmat + jnp.where(coef & (c == base + i), 1.0, 0.0)
        qmat = qmat + jnp.where(one & (c >= base + 3) & (c < base + 6), 1.0, 0.0)
    return kmat.astype(BF16), qmat.astype(BF16)


def _attn_kernel(*refs, kind, sq, skp, off, tq, tk, has_cache):
    it = iter(refs)
    q_ref, k_ref, v_ref = next(it), next(it), next(it)
    f_ref = next(it) if kind == "fox" else None
    ck_ref, cv_ref = (next(it), next(it)) if has_cache else (None, None)
    o_ref = next(it)
    kbuf, vbuf = (next(it), next(it)) if has_cache else (None, None)
    ka = (next(it), next(it)) if kind == "fox" else None
    va = (next(it), next(it)) if kind != "sb" else None

    if has_cache:
        past = ck_ref.shape[1]

        def fill(i, _):
            r0 = pl.multiple_of(i * ROW_CHUNK, ROW_CHUNK)
            kbuf[pl.ds(r0, ROW_CHUNK), :] = ck_ref[0, pl.ds(r0, ROW_CHUNK), :].astype(BF16)
            vbuf[pl.ds(r0, ROW_CHUNK), :] = cv_ref[0, pl.ds(r0, ROW_CHUNK), :].astype(BF16)
            return 0

        lax.fori_loop(0, past // ROW_CHUNK, fill, 0)
        kbuf[past:past + sq, :] = k_ref[0]
        vbuf[past:past + sq, :] = v_ref[0]
        if skp > past + sq:
            kbuf[past + sq:, :] = jnp.zeros((skp - past - sq, kbuf.shape[1]), BF16)
            vbuf[past + sq:, :] = jnp.zeros((skp - past - sq, LANES), BF16)
        kget = lambda r0, n: kbuf[pl.ds(r0, n), :]
        vget = lambda r0, n: vbuf[pl.ds(r0, n), :]
    else:
        kget = lambda r0, n: k_ref[0, pl.ds(r0, n), :]
        vget = lambda r0, n: v_ref[0, pl.ds(r0, n), :]

    if kind == "fox":
        kmat, qmat = _forget_placement(pl.program_id(1))
    if kind != "sb":
        lo_lanes = _half_mask((skp, LANES))
        vc = vget(0, skp)
        va[0][...] = jnp.where(lo_lanes, vc, jnp.ones_like(vc))
        va[1][...] = jnp.where(lo_lanes, jnp.ones_like(vc), vc)
        if kind == "fox":
            kc = kget(0, skp)
            e = _dot(f_ref[0], kmat).astype(BF16)
            ka[0][...] = jnp.where(lo_lanes, kc, e)
            ka[1][...] = jnp.where(lo_lanes, e, kc)

    nq = sq // tq
    row = lax.broadcasted_iota(jnp.int32, (tq, tk), 0)
    col = lax.broadcasted_iota(jnp.int32, (tq, tk), 1)
    lane_q = lax.broadcasted_iota(jnp.int32, (tq, LANES), 1)
    lo_half = lane_q < HEAD_DIM
    if kind == "sb":
        kr = lax.broadcasted_iota(jnp.int32, (2 * tk, tk), 0)
        kr = jnp.where(kr >= tk, kr - tk, kr)
        kc_ = lax.broadcasted_iota(jnp.int32, (2 * tk, tk), 1)
        later = jnp.where(kr > kc_, 1.0, 0.0).astype(BF16)

    def qblock(qi, _):
        q0 = qi * tq if isinstance(qi, int) else pl.multiple_of(qi * tq, tq)
        qblk = q_ref[0, pl.ds(q0, tq), :]
        qs = []
        if kind == "fox":
            eq = _dot(f_ref[0, pl.ds(off + q0, tq), :], qmat).astype(BF16)
        for hh in range(2):
            own = lo_half if hh == 0 else ~lo_half
            if kind == "mla":
                qs.append(qblk[:, hh * LANES:(hh + 1) * LANES])
            elif kind == "sb":
                qs.append(jnp.where(own, qblk, jnp.zeros_like(qblk)))
            else:
                qs.append(jnp.where(own, qblk, eq))

        def kload(hh, k0):
            if kind == "fox":
                return ka[hh][pl.ds(k0, tk), :]
            if kind == "mla":
                return kget(k0, tk)[:, hh * LANES:(hh + 1) * LANES]
            return kget(k0, tk)

        if kind == "sb":
            assert nq == 1
            nblk = skp // tk
            rowf = lax.broadcasted_iota(jnp.int32, (tq, skp), 0)
            colf = lax.broadcasted_iota(jnp.int32, (tq, skp), 1)
            vis = colf < (off + rowf)
            vall = vget(0, skp)
            outs = []
            for hh in range(2):
                z = _dot_nt(qs[hh], kget(0, skp))
                nl = jnp.where(vis, _softplus2(z), 0.0)
                hi = nl.astype(BF16)
                lo = (nl - hi.astype(F32)).astype(BF16)
                within = [_dot(jnp.concatenate([hi[:, j * tk:(j + 1) * tk], lo[:, j * tk:(j + 1) * tk]],
                                               axis=1), later) for j in range(nblk)]
                after = jnp.zeros((tq, 1), F32)
                parts = [None] * nblk
                for j in reversed(range(nblk)):
                    parts[j] = within[j] + after
                    after = after + within[j][:, 0:1] + nl[:, j * tk:j * tk + 1]
                suf = jnp.concatenate(parts, axis=1)
                a = jnp.where(vis, jnp.exp2(z - (nl + suf)), 0.0)
                outs.append(_dot(a.astype(BF16), vall))
        else:
            if kind == "fox":
                n_full = (off + q0 + 1) // tk
                n_total = (off + q0 + tq + tk - 1) // tk
            else:
                n_full = ((off + q0) // CHUNK * CHUNK + CHUNK) // tk
                n_total = (((off + q0 + tq - 1) // CHUNK + 1) * CHUNK + tk - 1) // tk

            def kstep(j, carry, masked):
                k0 = j * tk if isinstance(j, int) else pl.multiple_of(j * tk, tk)
                new = []
                for hh in range(2):
                    m, acc = carry[hh]
                    s = _dot_nt(qs[hh], kload(hh, k0))
                    if masked:
                        qpos = off + q0 + row
                        kpos = k0 + col
                        vis = (kpos <= qpos) if kind == "fox" else ((kpos // CHUNK) <= (qpos // CHUNK))
                        s = jnp.where(vis, s, NEG)
                    m_new = jnp.maximum(m, jnp.max(s, axis=-1, keepdims=True))
                    p = jnp.exp(s - m_new)
                    pv = _dot(p.astype(BF16), va[hh][pl.ds(k0, tk), :])
                    new.append((m_new, jnp.exp(m - m_new) * acc + pv))
                return tuple(new)

            init = ((jnp.full((tq, 1), NEG, F32), jnp.zeros((tq, LANES), F32)),) * 2
            carry = _loop(0, n_full, functools.partial(kstep, masked=False), init)
            carry = _loop(n_full, n_total, functools.partial(kstep, masked=True), carry)
            outs = [c[1] / pltpu.roll(c[1], HEAD_DIM, 1) for c in carry]
        o_ref[0, pl.ds(q0, tq), :] = jnp.where(lo_half, outs[0], outs[1]).astype(o_ref.dtype)
        return 0

    _loop(0, nq, qblock, 0) if nq == 1 else lax.fori_loop(0, nq, qblock, 0)


def _attention(kind, q, k, v, off, tq, tk, skp, f=None, cache=None):
    b, sq, qw = q.shape
    sn = k.shape[1]
    ql = qw // N_PAIRS
    pair = lambda n, w: pl.BlockSpec((1, n, w), lambda i, p: (i, 0, p))
    in_specs = [pair(sq, ql), pair(sn, ql), pair(sn, LANES)]
    args = [q, k, v]
    scratch = []
    if kind == "fox":
        in_specs.append(pl.BlockSpec((1, skp, LANES), lambda i, p: (i, 0, 0)))
        args.append(f)
    if cache is not None:
        ck, cv = cache
        in_specs += [pair(ck.shape[1], ql), pair(cv.shape[1], LANES)]
        args += [ck, cv]
        scratch += [pltpu.VMEM((skp, ql), BF16), pltpu.VMEM((skp, LANES), BF16)]
    else:
        assert skp == sn
    if kind == "fox":
        scratch += [pltpu.VMEM((skp, LANES), BF16)] * 2
    if kind != "sb":
        scratch += [pltpu.VMEM((skp, LANES), BF16)] * 2
    return pl.pallas_call(
        functools.partial(_attn_kernel, kind=kind, sq=sq, skp=skp, off=off, tq=tq, tk=tk,
                          has_cache=cache is not None),
        grid=(b, N_PAIRS),
        in_specs=in_specs,
        out_specs=pair(sq, LANES),
        out_shape=jax.ShapeDtypeStruct((b, sq, N_HEADS * HEAD_DIM), BF16),
        scratch_shapes=scratch,
        compiler_params=_cparams("parallel", "parallel"),
        name="attn_" + kind,
    )(*args)


def _attn_t_kernel(*refs, kind, sq, off, tq, tk):
    it = iter(refs)
    q_ref, k_ref, v_ref = next(it), next(it), next(it)
    f_ref = next(it) if kind == "fox" else None
    o_ref = next(it)
    ka = (next(it), next(it)) if kind == "fox" else None
    vt = (next(it), next(it)) if kind != "sb" else (next(it),)
    skp = k_ref.shape[1]

    lane_c = lax.broadcasted_iota(jnp.int32, (tk, LANES), 1)
    chan = lax.broadcasted_iota(jnp.int32, (LANES, tk), 0)
    if kind == "fox":
        kmat, qmat = _forget_placement(pl.program_id(1))

    def build(i, _):
        r0 = pl.multiple_of(i * tk, tk)
        vct = v_ref[0, pl.ds(r0, tk), :].astype(F32).T
        if kind == "sb":
            vt[0][:, pl.ds(r0, tk)] = vct.astype(BF16)
        else:
            vt[0][:, pl.ds(r0, tk)] = jnp.where(chan < HEAD_DIM, vct, 1.0).astype(BF16)
            vt[1][:, pl.ds(r0, tk)] = jnp.where(chan >= HEAD_DIM, vct, 1.0).astype(BF16)
        if kind == "fox":
            kc = k_ref[0, pl.ds(r0, tk), :]
            e = _dot(f_ref[0, pl.ds(r0, tk), :], kmat).astype(BF16)
            ka[0][pl.ds(r0, tk), :] = jnp.where(lane_c < HEAD_DIM, kc, e)
            ka[1][pl.ds(r0, tk), :] = jnp.where(lane_c >= HEAD_DIM, kc, e)
        return 0

    lax.fori_loop(0, skp // tk, build, 0)

    nq = sq // tq
    krow = lax.broadcasted_iota(jnp.int32, (tk, tq), 0)
    qcol = lax.broadcasted_iota(jnp.int32, (tk, tq), 1)
    lane_q = lax.broadcasted_iota(jnp.int32, (tq, LANES), 1)
    lo_half = lane_q < HEAD_DIM
    out_lo = lax.broadcasted_iota(jnp.int32, (LANES, tq), 0) < HEAD_DIM
    if kind == "sb":
        a_ = lax.broadcasted_iota(jnp.int32, (LANES, 2 * LANES), 0)
        b_ = lax.broadcasted_iota(jnp.int32, (LANES, 2 * LANES), 1)
        b_ = jnp.where(b_ >= LANES, b_ - LANES, b_)
        later_t = jnp.where(b_ > a_, 1.0, 0.0).astype(BF16)

    def qblock(qi):
        q0 = qi * tq
        qblk = q_ref[0, q0:q0 + tq, :]
        qs = []
        if kind == "fox":
            eq = _dot(f_ref[0, off + q0:off + q0 + tq, :], qmat).astype(BF16)
        for hh in range(2):
            own = lo_half if hh == 0 else ~lo_half
            if kind == "mla":
                qs.append(qblk[:, hh * LANES:(hh + 1) * LANES])
            elif kind == "sb":
                qs.append(jnp.where(own, qblk, jnp.zeros_like(qblk)))
            else:
                qs.append(jnp.where(own, qblk, eq))

        def kload(hh, k0):
            if kind == "fox":
                return ka[hh][k0:k0 + tk, :]
            if kind == "mla":
                return k_ref[0, k0:k0 + tk, hh * LANES:(hh + 1) * LANES]
            return k_ref[0, k0:k0 + tk, :]

        if kind == "sb":
            n_full = (off + q0) // tk
            n_total = (off + q0 + tq - 1 + tk - 1) // tk

            def stage1(j, masked):
                k0 = j * tk
                res = []
                for hh in range(2):
                    z = _dot_nt(kload(hh, k0), qs[hh])
                    nl = _softplus2(z)
                    if masked:
                        vis = (k0 + krow) < (off + q0 + qcol)
                        nl = jnp.where(vis, nl, 0.0)
                        z = jnp.where(vis, z, NEG)
                    hi = nl.astype(BF16)
                    res.append((z, nl, hi, (nl - hi.astype(F32)).astype(BF16)))
                return res

            def stage2(j, state, st1):
                vb = vt[0][:, j * tk:(j + 1) * tk]
                new = []
                for hh in range(2):
                    rem, acc = state[hh]
                    z, nl, hi, lo = st1[hh]
                    after = rem
                    parts = []
                    for sb in reversed(range(tk // LANES)):
                        rs = slice(sb * LANES, (sb + 1) * LANES)
                        within = _dot(later_t, jnp.concatenate([hi[rs], lo[rs]], axis=0))
                        parts.append(within + after)
                        after = after + within[0:1, :] + nl[sb * LANES:sb * LANES + 1, :]
                    suf = jnp.concatenate(parts[::-1], axis=0)
                    a = jnp.exp2(z - (nl + suf))
                    new.append((after, acc + _dot(vb, a.astype(BF16))))
                return new

            state = [(jnp.zeros((1, tq), F32), jnp.zeros((LANES, tq), F32))] * 2
            st1 = stage1(n_full, True)
            for j in range(n_full, -1, -1):
                nxt = stage1(j - 1, False) if j > 0 else None
                state = stage2(j, state, st1)
                st1 = nxt
            carry = state
            outs = [c[1] for c in carry]
        else:
            if kind == "fox":
                n_full = (off + q0 + 1) // tk
                n_total = (off + q0 + tq + tk - 1) // tk
            else:
                n_full = ((off + q0) // CHUNK * CHUNK + CHUNK) // tk
                n_total = (((off + q0 + tq - 1) // CHUNK + 1) * CHUNK + tk - 1) // tk

            def scores(j):
                return [_dot_nt(kload(hh, j * tk), qs[hh]) for hh in range(2)]

            s_cur = scores(0)
            m = [jnp.full((1, tq), NEG, F32)] * 2
            acc, p_prev, a_prev = [None, None], [None, None], [None, None]
            for j in range(n_total):
                s_next = scores(j + 1) if j + 1 < n_total else None
                for hh in range(2):
                    if j > 0:
                        pv = _dot(vt[hh][:, (j - 1) * tk:j * tk], p_prev[hh])
                        acc[hh] = pv if j == 1 else a_prev[hh] * acc[hh] + pv
                    s = s_cur[hh]
                    if j >= n_full:
                        qpos = off + q0 + qcol
                        kpos = j * tk + krow
                        vis = (kpos <= qpos) if kind == "fox" else ((kpos // CHUNK) <= (qpos // CHUNK))
                        s = jnp.where(vis, s, NEG)
                    m_new = jnp.maximum(m[hh], jnp.max(s, axis=0, keepdims=True))
                    p_prev[hh] = jnp.exp(s - m_new).astype(BF16)
                    a_prev[hh] = jnp.exp(m[hh] - m_new)
                    m[hh] = m_new
                s_cur = s_next
            accs = []
            for hh in range(2):
                pv = _dot(vt[hh][:, (n_total - 1) * tk:n_total * tk], p_prev[hh])
                accs.append(pv if n_total == 1 else a_prev[hh] * acc[hh] + pv)
            outs = [accs[0] / accs[0][HEAD_DIM:HEAD_DIM + 1, :], accs[1] / accs[1][0:1, :]]
        o_ref[0, q0:q0 + tq, :] = jnp.where(out_lo, outs[0], outs[1]).T.astype(o_ref.dtype)

    for qi in range(nq):
        qblock(qi)


def _attention_t(kind, q, k, v, off, tq, tk, f=None):
    b, sq, qw = q.shape
    skp = k.shape[1]
    ql = qw // N_PAIRS
    pair = lambda n, w: pl.BlockSpec((1, n, w), lambda i, p: (i, 0, p))
    in_specs = [pair(sq, ql), pair(skp, ql), pair(skp, LANES)]
    args = [q, k, v]
    scratch = []
    if kind == "fox":
        in_specs.append(pl.BlockSpec((1, skp, LANES), lambda i, p: (i, 0, 0)))
        args.append(f)
        scratch += [pltpu.VMEM((skp, LANES), BF16)] * 2
    scratch += [pltpu.VMEM((LANES, skp), BF16)] * (1 if kind == "sb" else 2)
    return pl.pallas_call(
        functools.partial(_attn_t_kernel, kind=kind, sq=sq, off=off, tq=tq, tk=tk),
        grid=(b, N_PAIRS),
        in_specs=in_specs,
        out_specs=pair(sq, LANES),
        out_shape=jax.ShapeDtypeStruct((b, sq, N_HEADS * HEAD_DIM), BF16),
        scratch_shapes=scratch,
        compiler_params=_cparams("parallel", "parallel"),
        name="attn_t_" + kind,
    )(*args)


SWA_TQ = 2 * CHUNK


def _swa_span(tq):
    return _round_up(tq + WINDOW, LANES)


def _swa_kernel(slope_ref, sink_ref, q_ref, k_ref, v_ref, o_ref, vta, vtb, *, sq, past, tq):
    nq = sq // tq
    kv = pl.program_id(1)
    sk = k_ref.shape[1]
    rows = 4 * ROW_CHUNK if sk % (4 * ROW_CHUNK) == 0 else ROW_CHUNK
    chan = lax.broadcasted_iota(jnp.int32, (LANES, rows), 0)

    def build(i, _):
        r0 = pl.multiple_of(i * rows, rows)
        vct = v_ref[0, pl.ds(r0, rows), :].astype(F32).T
        vta[:, pl.ds(r0, rows)] = jnp.where(chan < HEAD_DIM, vct, 1.0).astype(BF16)
        vtb[:, pl.ds(r0, rows)] = jnp.where(chan >= HEAD_DIM, vct, 1.0).astype(BF16)
        return 0

    lax.fori_loop(0, sk // rows, build, 0)

    span = _swa_span(tq)
    krow = lax.broadcasted_iota(jnp.int32, (span, tq), 0)
    qcol = lax.broadcasted_iota(jnp.int32, (span, tq), 1)
    lo_half = _half_mask((tq, LANES))
    out_lo = lax.broadcasted_iota(jnp.int32, (LANES, tq), 0) < HEAD_DIM

    def qblocks(q0s):
        work = []
        for q0 in q0s:
            ks = jnp.maximum(q0 + past - WINDOW, 0)
            ks = ks if isinstance(ks, int) else pl.multiple_of(ks, LANES)
            kwin = k_ref[0, pl.ds(ks, span), :]
            for pair in range(SWA_GROUPS // 2):
                qpair = q_ref[0, pl.ds(q0, tq), pair * LANES:(pair + 1) * LANES]
                for half in range(2):
                    qh = jnp.where(lo_half if half == 0 else ~lo_half, qpair, jnp.zeros_like(qpair))
                    work.append(dict(q0=q0, ks=ks, pair=pair, half=half, s=_dot_nt(kwin, qh)))
        for w in work:
            q0, ks = w["q0"], w["ks"]
            head = kv * SWA_GROUPS + w["pair"] * 2 + w["half"]
            qpos = q0 + qcol
            kpos = ks - past + krow
            qc = qpos // CHUNK
            vis = (kpos >= qc * CHUNK - WINDOW) & (kpos < (qc + 1) * CHUNK)
            dist = jnp.abs(qpos - kpos).astype(F32)
            s = jnp.where(vis, w["s"] - slope_ref[head] * dist, NEG)
            sink = sink_ref[head]
            m = jnp.maximum(jnp.max(s, axis=0, keepdims=True), sink)
            w["e"] = jnp.exp(s - m).astype(BF16)
            w["tail"] = jnp.exp(sink - m)
        for w in work:
            vt = vta if w["half"] == 0 else vtb
            w["ot"] = _dot(vt[:, pl.ds(w["ks"], span)], w["e"])
        for i in range(0, len(work), 2):
            lo, hi = work[i], work[i + 1]
            o_lo = lo["ot"] / (lo["ot"][HEAD_DIM:HEAD_DIM + 1, :] + lo["tail"])
            o_hi = hi["ot"] / (hi["ot"][0:1, :] + hi["tail"])
            o_ref[0, pl.ds(lo["q0"], tq), lo["pair"] * LANES:(lo["pair"] + 1) * LANES] = (
                jnp.where(out_lo, o_lo, o_hi).T.astype(o_ref.dtype))

    if nq == 1:
        qblocks([0])
    else:
        group = 4 if nq % 4 == 0 else 2

        def body(i, _):
            q0 = pl.multiple_of(i * (group * tq), group * tq)
            qblocks([q0 + g * tq for g in range(group)])
            return 0

        lax.fori_loop(0, nq // group, body, 0)


def _swa_attn(q, kdup, vdup, slopes, sinks, past, tq):
    b, sq, _ = q.shape
    sk = kdup.shape[1]
    gw = SWA_GROUPS * HEAD_DIM
    smem = pl.BlockSpec(memory_space=pltpu.SMEM)
    return pl.pallas_call(
        functools.partial(_swa_kernel, sq=sq, past=past, tq=tq),
        grid=(b, SWA_KV_HEADS),
        in_specs=[smem, smem,
                  pl.BlockSpec((1, sq, gw), lambda i, p: (i, 0, p)),
                  pl.BlockSpec((1, sk, LANES), lambda i, p: (i, 0, p)),
                  pl.BlockSpec((1, sk, LANES), lambda i, p: (i, 0, p))],
        out_specs=pl.BlockSpec((1, sq, gw), lambda i, p: (i, 0, p)),
        out_shape=jax.ShapeDtypeStruct((b, sq, N_HEADS * HEAD_DIM), BF16),
        scratch_shapes=[pltpu.VMEM((LANES, sk), BF16)] * 2,
        compiler_params=_cparams("parallel", "parallel"),
        name="attn_swa",
    )(slopes, sinks, q, kdup, vdup)


MLA_IN_COLS = MLA_Q_RANK + MLA_KV_RANK + 2 * LANES
MLA_HEAD_W = LANES


def _rms_norm(x, g):
    return x * lax.rsqrt(jnp.mean(x * x, axis=-1, keepdims=True) + 1e-6) * g


def _mla_proj_kernel(x_ref, sc_ref, sh_ref, win_ref, gq_ref, gkv_ref, wqa_ref, wqb_ref,
                     wk_ref, wv_ref, qc_ref, qs_ref, kc_ref, ks_ref,
                     q_ref, kf_ref, v_ref, ckv_ref, kr_ref):
    h = (x_ref[0] * (1.0 + sc_ref[0]) + sh_ref[0]).astype(BF16)
    proj = _dot(h, win_ref[...])
    cq = _rms_norm(proj[:, :MLA_Q_RANK], gq_ref[...]).astype(BF16)
    ckv = _rms_norm(proj[:, MLA_Q_RANK:MLA_Q_RANK + MLA_KV_RANK], gkv_ref[...])
    ckv_ref[0] = ckv
    ckv_b = ckv.astype(BF16)
    kr0 = MLA_Q_RANK + MLA_KV_RANK
    krr = proj[:, kr0:kr0 + LANES] * kc_ref[...] + proj[:, kr0 + LANES:kr0 + 2 * LANES] * ks_ref[...]
    kr_ref[0] = pltpu.roll(krr, LANES - MLA_D_NOPE, 1)[:, :MLA_D_ROPE]
    qa = _dot(cq, wqa_ref[...])
    qb = _dot(cq, wqb_ref[...])
    kn = _dot(ckv_b, wk_ref[...])
    v_ref[0] = _dot(ckv_b, wv_ref[...]).astype(BF16)
    qc, qs = qc_ref[...], qs_ref[...]
    for hd in range(N_HEADS):
        sl = slice(hd * MLA_HEAD_W, (hd + 1) * MLA_HEAD_W)
        q_ref[0, :, sl] = (qa[:, sl] * qc + qb[:, sl] * qs).astype(BF16)
        kf_ref[0, :, sl] = (kn[:, sl] + krr).astype(BF16)


def _mla_proj(x, sc, sh, mw, tables, tm=256):
    b, s, d = x.shape
    tm = _row_tile(s, tm)
    qc, qs, kc, ks = tables
    full = lambda a: pl.BlockSpec(a.shape, lambda i, j: (0,) * a.ndim)
    tab = pl.BlockSpec((tm, LANES), lambda i, j: (j, 0))
    row = lambda n: pl.BlockSpec((1, tm, n), lambda i, j: (i, j, 0))
    hw = N_HEADS * MLA_HEAD_W
    return pl.pallas_call(
        _mla_proj_kernel,
        grid=(b, s // tm),
        in_specs=[row(d),
                  pl.BlockSpec((1, 1, d), lambda i, j: (i, 0, 0)),
                  pl.BlockSpec((1, 1, d), lambda i, j: (i, 0, 0)),
                  full(mw["win"]), full(mw["gq"]), full(mw["gkv"]), full(mw["wqa"]),
                  full(mw["wqb"]), full(mw["wk"]), full(mw["wv"]), tab, tab, tab, tab],
        out_specs=[row(hw), row(hw), row(N_HEADS * HEAD_DIM), row(MLA_KV_RANK), row(MLA_D_ROPE)],
        out_shape=[jax.ShapeDtypeStruct((b, s, hw), BF16),
                   jax.ShapeDtypeStruct((b, s, hw), BF16),
                   jax.ShapeDtypeStruct((b, s, N_HEADS * HEAD_DIM), BF16),
                   jax.ShapeDtypeStruct((b, s, MLA_KV_RANK), F32),
                   jax.ShapeDtypeStruct((b, s, MLA_D_ROPE), F32)],
        compiler_params=_cparams("parallel", "parallel"),
        name="proj_mla",
    )(x, sc, sh, mw["win"], mw["gq"], mw["gkv"], mw["wqa"], mw["wqb"], mw["wk"], mw["wv"],
      qc, qs, kc, ks)


def _mla_cache_kernel(ckv_ref, kr_ref, wk_ref, wv_ref, place_ref, kf_ref, v_ref):
    ckv_b = ckv_ref[0].astype(BF16)
    kf_ref[0] = (_dot(ckv_b, wk_ref[...]) + _dot(kr_ref[0], place_ref[...])).astype(BF16)
    v_ref[0] = _dot(ckv_b, wv_ref[...]).astype(BF16)


def _mla_cache_up(ckv, kr_pad, mw, tm=512):
    b, p, r = ckv.shape
    tm = _row_tile(p, tm)
    hw = N_HEADS * MLA_HEAD_W
    full = lambda a: pl.BlockSpec(a.shape, lambda i, j: (0,) * a.ndim)
    row = lambda n: pl.BlockSpec((1, tm, n), lambda i, j: (i, j, 0))
    return pl.pallas_call(
        _mla_cache_kernel,
        grid=(b, p // tm),
        in_specs=[row(r), row(LANES), full(mw["wk"]), full(mw["wv"]), full(mw["place"])],
        out_specs=[row(hw), row(N_HEADS * HEAD_DIM)],
        out_shape=[jax.ShapeDtypeStruct((b, p, hw), BF16),
                   jax.ShapeDtypeStruct((b, p, N_HEADS * HEAD_DIM), BF16)],
        compiler_params=_cparams("parallel", "parallel"),
        name="mla_cache_up",
    )(ckv, kr_pad, mw["wk"], mw["wv"], mw["place"])


def _xor_partner(x, k, idx, axis):
    n = x.shape[axis]
    up = pltpu.roll(x, n - k, axis)
    dn = pltpu.roll(x, k, axis)
    return jnp.where((idx & k) == 0, up, dn)


def _argmax_groups(v, idx, pos, strides, axis):
    for k in strides:
        pv = _xor_partner(v, k, pos, axis)
        pi = _xor_partner(idx, k, pos, axis)
        take = (pv > v) | ((pv == v) & (pi < idx))
        v = jnp.where(take, pv, v)
        idx = jnp.where(take, pi, idx)
    return v, idx


def _route(logits, rb):
    ex = lax.broadcasted_iota(jnp.int32, logits.shape, 0)
    scores = _sigmoid(logits)
    biased = scores + rb
    p1 = _xor_partner(biased, 1, ex, 0)
    hi1, lo1 = jnp.maximum(biased, p1), jnp.minimum(biased, p1)
    hi2, lo2 = _xor_partner(hi1, 2, ex, 0), _xor_partner(lo1, 2, ex, 0)
    group_score = jnp.maximum(hi1, hi2) + jnp.maximum(jnp.minimum(hi1, hi2), jnp.maximum(lo1, lo2))
    gid = ex >> 2
    _, best = _argmax_groups(group_score, gid, ex, (4, 8), 0)
    cand = jnp.where(gid == best, biased, NEG)
    _, i1 = _argmax_groups(cand, ex, ex, (1, 2, 4, 8), 0)
    _, i2 = _argmax_groups(jnp.where(ex == i1, -jnp.inf, cand), ex, ex, (1, 2, 4, 8), 0)
    sel = jnp.where((ex == i1) | (ex == i2), scores, 0.0)
    return sel / jnp.sum(sel, axis=0, keepdims=True)


def _outproj_kernel(o_ref, x_ref, w_ref, ga_ref, scf_ref, shf_ref, g_ref, b_ref,
                    rwh_ref, rwl_ref, rb_ref, x1_ref, h2_ref, gate_ref):
    y = DEEPNORM_ALPHA * x_ref[0] + (1.0 + ga_ref[0]) * _dot(o_ref[0], w_ref[...])
    x1 = _layer_norm(y, g_ref[...], b_ref[...])
    x1_ref[0] = x1
    h2 = x1 * (1.0 + scf_ref[0]) + shf_ref[0]
    hi = h2.astype(BF16)
    h2_ref[0] = hi
    lo = (h2 - hi.astype(F32)).astype(BF16)
    logits = _dot_nt(rwh_ref[...], hi) + _dot_nt(rwh_ref[...], lo) + _dot_nt(rwl_ref[...], hi)
    gates = _route(logits, rb_ref[...])
    pad = jnp.zeros((LANES - N_EXPERTS, gates.shape[1]), F32)
    gate_ref[0] = jnp.concatenate([gates, pad], axis=0).T[:, :N_EXPERTS]


def _outproj(o, x, w, ga, scf, shf, g, bta, rwh, rwl, rb, tm=512):
    b, s, d = x.shape
    tm = _row_tile(s, tm)
    row = lambda n: pl.BlockSpec((1, tm, n), lambda i, j: (i, j, 0))
    mod = pl.BlockSpec((1, 1, d), lambda i, j: (i, 0, 0))
    full = lambda a: pl.BlockSpec(a.shape, lambda i, j: (0,) * a.ndim)
    return pl.pallas_call(
        _outproj_kernel,
        grid=(b, s // tm),
        in_specs=[row(d), row(d), full(w), mod, mod, mod, full(g), full(bta),
                  full(rwh), full(rwl), full(rb)],
        out_specs=[row(d), row(d), row(N_EXPERTS)],
        out_shape=[jax.ShapeDtypeStruct((b, s, d), F32), jax.ShapeDtypeStruct((b, s, d), BF16),
                   jax.ShapeDtypeStruct((b, s, N_EXPERTS), F32)],
        compiler_params=_cparams("parallel", "parallel"),
        name="outproj_ln_router",
    )(o, x, w, ga, scf, shf, g, bta, rwh, rwl, rb)


EXPERT_UNROLL = 4


def _moe_kernel(h_ref, gate_ref, x_ref, gf_ref, g_ref, b_ref, wg_ref, wu_ref, wd_ref,
                o_ref, acc_ref):
    bt, tm, d = h_ref.shape
    rows = bt * tm
    ne, _, de = wg_ref.shape
    h = h_ref[...].reshape(rows, d)
    gates = gate_ref[...].reshape(rows, N_EXPERTS)
    lane = lax.broadcasted_iota(jnp.int32, gates.shape, 1)
    acc_ref[...] = jnp.zeros_like(acc_ref)

    def body(i, _):
        e0 = i * EXPERT_UNROLL
        acts = []
        for u in range(EXPERT_UNROLL):
            e = e0 + u
            hg = _dot(h, wg_ref[e])
            hu = _dot(h, wu_ref[e])
            ge = jnp.sum(jnp.where(lane == e, gates, 0.0), axis=-1, keepdims=True)
            acts.append((hg * _sigmoid(hg) * hu * ge).astype(BF16))
        wd = wd_ref[pl.ds(e0, EXPERT_UNROLL)].reshape(EXPERT_UNROLL * de, d)
        acc_ref[...] += _dot(jnp.concatenate(acts, axis=1), wd)
        return 0

    lax.fori_loop(0, ne // EXPERT_UNROLL, body, 0)
    y = DEEPNORM_ALPHA * x_ref[...] + (1.0 + gf_ref[...]) * acc_ref[...].reshape(bt, tm, d)
    o_ref[...] = _layer_norm(y, g_ref[...], b_ref[...])


def _moe(h2, gates, x1, gf, g, bta, wg, wu, wd, rows=512):
    b, s, d = x1.shape
    tm = _row_tile(s, rows)
    bt = min(b, rows // tm)
    row = lambda n: pl.BlockSpec((bt, tm, n), lambda i, j: (i, j, 0))
    full = lambda a: pl.BlockSpec(a.shape, lambda i, j: (0,) * a.ndim)
    once = lambda a: pl.BlockSpec(a.shape, lambda i, j: (0,) * a.ndim, pipeline_mode=pl.Buffered(1))
    return pl.pallas_call(
        _moe_kernel,
        grid=(b // bt, s // tm),
        in_specs=[row(d), row(N_EXPERTS), row(d),
                  pl.BlockSpec((bt, 1, d), lambda i, j: (i, 0, 0)), full(g), full(bta),
                  once(wg), once(wu), once(wd)],
        out_specs=row(d),
        out_shape=jax.ShapeDtypeStruct((b, s, d), F32),
        scratch_shapes=[pltpu.VMEM((bt * tm, d), F32)],
        compiler_params=_cparams("parallel", "parallel"),
        name="moe_ffn",
    )(h2, gates, x1, gf, g, bta, wg, wu, wd)


def _pad_cols(w, n):
    return jnp.pad(w, ((0, 0), (0, n - w.shape[1])))


def _rot_half_cols(w):
    half = w.shape[-1] // 2
    return jnp.concatenate([-w[..., half:], w[..., :half]], axis=-1)


def _mla_weights(mla_w_in, mla_q_norm, mla_w_uq, mla_kv_norm, mla_w_uk, mla_w_uv):
    d = mla_w_in.shape[0]
    kr0 = MLA_Q_RANK + MLA_KV_RANK
    wkr = mla_w_in[:, kr0:]
    z = lambda n: jnp.zeros((d, n), F32)
    win = jnp.concatenate([mla_w_in[:, :kr0], z(MLA_D_NOPE), wkr, z(LANES - MLA_D_NOPE - MLA_D_ROPE),
                           z(MLA_D_NOPE), _rot_half_cols(wkr), z(LANES - MLA_D_NOPE - MLA_D_ROPE)],
                          axis=1)
    wuq = mla_w_uq.reshape(MLA_Q_RANK, N_HEADS, MLA_D_NOPE + MLA_D_ROPE)
    zq = lambda n: jnp.zeros((MLA_Q_RANK, N_HEADS, n), F32)
    tail = LANES - MLA_D_NOPE - MLA_D_ROPE
    wqa = jnp.concatenate([wuq, zq(tail)], axis=-1).reshape(MLA_Q_RANK, N_HEADS * LANES)
    wqb = jnp.concatenate([zq(MLA_D_NOPE), _rot_half_cols(wuq[..., MLA_D_NOPE:]), zq(tail)],
                          axis=-1).reshape(MLA_Q_RANK, N_HEADS * LANES)
    wk = jnp.concatenate([mla_w_uk, jnp.zeros((MLA_KV_RANK, N_HEADS, LANES - MLA_D_NOPE), F32)],
                         axis=-1).reshape(MLA_KV_RANK, N_HEADS * LANES)
    wv = mla_w_uv.reshape(MLA_KV_RANK, N_HEADS * HEAD_DIM)
    src = jnp.arange(LANES)[:, None]
    dst = jnp.arange(N_HEADS * LANES)[None, :]
    place = ((dst % LANES) == src + MLA_D_NOPE) & (src < MLA_D_ROPE)
    return dict(win=win.astype(BF16), gq=mla_q_norm.reshape(1, -1), gkv=mla_kv_norm.reshape(1, -1),
                wqa=wqa.astype(BF16), wqb=wqb.astype(BF16), wk=wk.astype(BF16),
                wv=wv.astype(BF16), place=place.astype(BF16))


def _rope_tables(n_pos):
    half = MLA_D_ROPE // 2
    inv = ROPE_THETA ** (-jnp.arange(half, dtype=F32) * 2.0 / MLA_D_ROPE)
    ang = jnp.arange(n_pos, dtype=F32)[:, None] * inv[None, :]
    cos = jnp.concatenate([jnp.cos(ang), jnp.cos(ang)], axis=-1)
    sin = jnp.concatenate([jnp.sin(ang), jnp.sin(ang)], axis=-1)
    scale = (MLA_D_NOPE + MLA_D_ROPE) ** -0.5
    z = lambda n: jnp.zeros((n_pos, n), F32)
    tail = LANES - MLA_D_NOPE - MLA_D_ROPE
    qc = jnp.concatenate([jnp.full((n_pos, MLA_D_NOPE), scale, F32), cos * scale, z(tail)], axis=1)
    qs = jnp.concatenate([z(MLA_D_NOPE), sin * scale, z(tail)], axis=1)
    kc = jnp.concatenate([z(MLA_D_NOPE), cos, z(tail)], axis=1)
    ks = jnp.concatenate([z(MLA_D_NOPE), sin, z(tail)], axis=1)
    return qc, qs, kc, ks


def _dup_heads(w):
    d = w.shape[0]
    w4 = w.reshape(d, SWA_KV_HEADS, HEAD_DIM)
    return jnp.concatenate([w4, w4], axis=-1).reshape(d, SWA_KV_HEADS * LANES)


def _pad_rows(a, n):
    return jnp.pad(a, ((0, 0), (0, n - a.shape[1]), (0, 0)))


def _round_up(n, m):
    return -(-n // m) * m


def _attn_tiles(kind, sq, n_keys):
    if sq % LONG_TQ == 0:
        return LONG_TQ, LONG_TQ, n_keys
    if kind == "sb":
        return sq, 256, _round_up(n_keys, 256)
    skp = _round_up(n_keys, LANES)
    return sq, skp, skp


def _flat_heads(a):
    return a.reshape(a.shape[0], a.shape[1], -1)


def _fox_mixer(x, sc, sh, fw, cache):
    b, s, _ = x.shape
    q, k32, kb, v32, vb, logf = _proj(x, sc, sh, fw["w"], [[BF16], [F32, BF16], [F32, BF16], ["logf"]],
                                      bias=fw["bias"], name="proj_fox")
    off = 0 if cache is None else cache[0].shape[1]
    tq, tk, skp = _attn_tiles("fox", s, off + s)
    lall = logf if cache is None else jnp.concatenate([cache[2], logf], axis=1)
    f = _forget_cumsum(_pad_rows(lall, skp))
    kv_cache = None if cache is None else (_flat_heads(cache[0]), _flat_heads(cache[1]))
    if cache is None:
        o = _attention_t("fox", q, kb, vb, off, tq, tk, f=f)
    else:
        o = _attention("fox", q, kb, vb, off, tq, tk, skp, f=f, cache=kv_cache)
    hshape = (b, s, N_HEADS, HEAD_DIM)
    return o, (k32.reshape(hshape), v32.reshape(hshape), logf)


def _mla_mixer(x, sc, sh, mw, cache):
    b, s, _ = x.shape
    off = 0 if cache is None else cache[0].shape[1]
    tables = [t[off:off + s] for t in mw["tables"]]
    q, kf, v, ckv, kr = _mla_proj(x, sc, sh, mw, tables)
    tq, tk, skp = _attn_tiles("mla", s, off + s)
    kv_cache = None
    if cache is not None:
        c_ckv, c_kr = cache
        kr_pad = jnp.pad(c_kr, ((0, 0), (0, 0), (0, LANES - MLA_D_ROPE))).astype(BF16)
        kv_cache = _mla_cache_up(c_ckv, kr_pad, mw)
    if cache is None:
        o = _attention_t("mla", q, kf, v, off, tq, tk)
    else:
        o = _attention("mla", q, kf, v, off, tq, tk, skp, cache=kv_cache)
    return o, (ckv, kr)


def _swa_mixer(x, sc, sh, sw, cache):
    b, s, _ = x.shape
    q, k32, kdup, v32, vdup = _proj(x, sc, sh, sw["w"], [[BF16], [F32], [BF16], [F32], [BF16]],
                                    name="proj_swa")
    kvshape = lambda a: a.reshape(a.shape[0], a.shape[1], SWA_KV_HEADS, HEAD_DIM)
    if cache is None:
        past, tq = 0, min(s, SWA_TQ)
        new_k, new_v = kvshape(k32[:, -WINDOW:]), kvshape(v32[:, -WINDOW:])
    else:
        ck, cv = cache
        past, tq = ck.shape[1], s
        dup = lambda c: jnp.concatenate([c, c], axis=-1).reshape(c.shape[0], c.shape[1], -1).astype(BF16)
        kdup = _pad_rows(jnp.concatenate([dup(ck), kdup], axis=1), _swa_span(tq))
        vdup = _pad_rows(jnp.concatenate([dup(cv), vdup], axis=1), _swa_span(tq))
        new_k = jnp.concatenate([ck, kvshape(k32)], axis=1)[:, -WINDOW:]
        new_v = jnp.concatenate([cv, kvshape(v32)], axis=1)[:, -WINDOW:]
    o = _swa_attn(q, kdup, vdup, sw["slopes"], sw["sinks"], past, tq)
    return o, (new_k, new_v)


def _sb_mixer(x, sc, sh, bw, cache):
    b, s, _ = x.shape
    q, k32, kb, v32, vb = _proj(x, sc, sh, bw["w"], [[BF16], [F32, BF16], [F32, BF16]], name="proj_sb")
    off = 0 if cache is None else cache[0].shape[1]
    tq, tk, skp = _attn_tiles("sb", s, off + s)
    kv_cache = None if cache is None else (_flat_heads(cache[0]), _flat_heads(cache[1]))
    if cache is None:
        o = _attention_t("sb", q, kb, vb, off, tq, tk)
    else:
        o = _attention("sb", q, kb, vb, off, tq, tk, skp, cache=kv_cache)
    hshape = (b, s, N_HEADS, HEAD_DIM)
    return o, (k32.reshape(hshape), v32.reshape(hshape))


_MIXERS = (_fox_mixer, _mla_mixer, _swa_mixer, _sb_mixer)


def _trunk(x, mods, caches, wts):
    states = []
    for i in range(DEPTH):
        m = [mods[i, :, j][:, None, :] for j in range(6)]
        sh_a, sc_a, g_a, sh_f, sc_f, g_f = m
        o, st = _MIXERS[i](x, sc_a, sh_a, wts["mixer"][i], caches[i])
        states.extend(st)
        x1, h2, gates = _outproj(o, x, wts["w_out"][i], g_a, sc_f, sh_f,
                                 wts["ln_g"][i, 0:1], wts["ln_b"][i, 0:1],
                                 wts["rw_hi"], wts["rw_lo"], wts["rb"])
        x = _moe(h2, gates, x1, g_f, wts["ln_g"][i, 1:2], wts["ln_b"][i, 1:2],
                 wts["moe_g"][i], wts["moe_u"][i], wts["moe_d"][i])
    return x, states


def kernel(x_prompt, x_sample, cache_fox_k, cache_fox_v, cache_fox_logf, cache_mla_ckv, cache_mla_krope, cache_swa_k, cache_swa_v, cache_sb_k, cache_sb_v, c_prompt, c_sample, ada_w, ada_b, ln_g, ln_b, fox_w_in, fox_b_f, fox_w_out, mla_w_in, mla_q_norm, mla_w_uq, mla_kv_norm, mla_w_uk, mla_w_uv, mla_w_out, swa_w_in, swa_sinks, swa_w_out, sb_w_in, sb_w_out, router_w, router_b, moe_w_gate, moe_w_up, moe_w_down):
    bp, sp, d = x_prompt.shape
    bs, ss, _ = x_sample.shape
    past = cache_fox_k.shape[1]
    hw = N_HEADS * HEAD_DIM
    qscale = HEAD_DIM ** -0.5

    fox = dict(w=[(fox_w_in[:, :hw] * qscale).astype(BF16), fox_w_in[:, hw:2 * hw].astype(BF16),
                  fox_w_in[:, 2 * hw:3 * hw].astype(BF16), _pad_cols(fox_w_in[:, 3 * hw:], LANES).astype(BF16)],
               bias=_pad_cols(fox_b_f.reshape(1, -1), LANES))
    mla = _mla_weights(mla_w_in, mla_q_norm, mla_w_uq, mla_kv_norm, mla_w_uk, mla_w_uv)
    mla["tables"] = _rope_tables(max(sp, past + ss))
    kvw = SWA_KV_HEADS * HEAD_DIM
    wk, wv = swa_w_in[:, hw:hw + kvw], swa_w_in[:, hw + kvw:]
    swa = dict(w=[(swa_w_in[:, :hw] * qscale).astype(BF16), wk.astype(BF16), _dup_heads(wk).astype(BF16),
                  wv.astype(BF16), _dup_heads(wv).astype(BF16)],
               slopes=jnp.exp2(-8.0 * jnp.arange(1, N_HEADS + 1, dtype=F32) / N_HEADS),
               sinks=swa_sinks.astype(F32))
    sb = dict(w=[(sb_w_in[:, :hw] * (qscale * LOG2E)).astype(BF16), sb_w_in[:, hw:2 * hw].astype(BF16),
                 sb_w_in[:, 2 * hw:].astype(BF16)])
    rw = router_w.T
    rw_hi = rw.astype(BF16)
    wts = dict(mixer=(fox, mla, swa, sb),
               w_out=[w.astype(BF16) for w in (fox_w_out, mla_w_out, swa_w_out, sb_w_out)],
               ln_g=ln_g, ln_b=ln_b, rw_hi=rw_hi, rw_lo=(rw - rw_hi.astype(F32)).astype(BF16),
               rb=router_b.reshape(-1, 1),
               moe_g=moe_w_gate.astype(BF16), moe_u=moe_w_up.astype(BF16), moe_d=moe_w_down.astype(BF16))

    mods = _ada_mod(jnp.concatenate([c_prompt, c_sample], axis=0), ada_w, ada_b)
    mods_p = mods[:, :bp].reshape(DEPTH, bp, 6, d)
    mods_s = mods[:, bp:].reshape(DEPTH, bs, 6, d)

    y_p, st_p = _trunk(x_prompt, mods_p, (None, None, None, None), wts)
    caches = ((cache_fox_k, cache_fox_v, cache_fox_logf), (cache_mla_ckv, cache_mla_krope),
              (cache_swa_k, cache_swa_v), (cache_sb_k, cache_sb_v))
    y_s, st_s = _trunk(x_sample, mods_s, caches, wts)
    fox_k_p, fox_v_p, fox_logf_p, mla_ckv_p, mla_krope_p, swa_k_p, swa_v_p, sb_k_p, sb_v_p = st_p
    fox_k_s, fox_v_s, fox_logf_s, mla_ckv_s, mla_krope_s, swa_k_s, swa_v_s, sb_k_s, sb_v_s = st_s
    return (y_p, y_s, fox_k_p, fox_k_s, fox_v_p, fox_v_s, fox_logf_p, fox_logf_s,
            mla_ckv_p, mla_ckv_s, mla_krope_p, mla_krope_s, swa_k_p, swa_k_s, swa_v_p, swa_v_s,
            sb_k_p, sb_k_s, sb_v_p, sb_v_s)
```
